```python
import math
import jax, jax.numpy as jnp
from jax import lax
import numpy as np

D_MODEL = 2048
BATCH = 4
SEQ = 2048
DEPTH = 2

N_A_LAYERS = (DEPTH + 1) // 2
N_B_LAYERS = DEPTH - N_A_LAYERS
N_DENSE_LAYERS = (DEPTH + 1) // 2
N_MOE_LAYERS = DEPTH // 2

GDN_HEAD_DIM = 128
GDN_QK_HEADS = D_MODEL // 128
GDN_V_HEADS = 2 * GDN_QK_HEADS
GDN_KEY_DIM = GDN_QK_HEADS * GDN_HEAD_DIM
GDN_VALUE_DIM = GDN_V_HEADS * GDN_HEAD_DIM
GDN_CONV_DIM = 2 * GDN_KEY_DIM + GDN_VALUE_DIM
GDN_IN_DIM = GDN_CONV_DIM + GDN_VALUE_DIM + 2 * GDN_V_HEADS
CONV_WIDTH = 4
CHUNK = 64

SB_HEADS = D_MODEL // 128
SB_HEAD_DIM = 128
SB_DIM = SB_HEADS * SB_HEAD_DIM
SB_BLOCK = 128

FFN_DIM = 7 * D_MODEL // 2
N_EXPERTS = 8
TOP_K = 2

DN_ALPHA = (2.0 * DEPTH) ** 0.25
DN_BETA = (8.0 * DEPTH) ** -0.25
LN_EPS = 1e-5
RMS_EPS = 1e-6
L2_EPS = 1e-6

kernel_name = 'hybrid_gdn_stickbreak_yoco_moe'


def layer_norm(x, g, b):
    xf = x.astype(jnp.float32)
    mu = jnp.mean(xf, axis=-1, keepdims=True)
    var = jnp.mean(jnp.square(xf - mu), axis=-1, keepdims=True)
    return ((xf - mu) * lax.rsqrt(var + LN_EPS) * g + b).astype(x.dtype)


def ada_mod(c, w, b):
    return jax.nn.silu(c) @ w + b


def modulate(x, shift, scale):
    return x * (1 + scale[:, None, :]) + shift[:, None, :]


def l2norm(x):
    xf = x.astype(jnp.float32)
    return xf * lax.rsqrt(jnp.sum(xf * xf, axis=-1, keepdims=True) + L2_EPS)


def causal_depthwise_conv(x, w):
    taps = jnp.transpose(w)[:, None, :].astype(x.dtype)
    return lax.conv_general_dilated(x, taps, window_strides=(1,), padding=[(CONV_WIDTH - 1, 0)],
                                    dimension_numbers=('NWC', 'WIO', 'NWC'),
                                    feature_group_count=x.shape[-1])


def chunk_gated_delta_rule(q, k, v, g, beta):
    B, T, H, dk = q.shape
    dv = v.shape[-1]
    nc = T // CHUNK
    f32 = jnp.float32

    def to_chunks(t):
        t = t.astype(f32).reshape((B, nc, CHUNK) + t.shape[2:])
        return jnp.moveaxis(t, 3, 1)

    q = to_chunks(q) * dk ** -0.5
    k, v, g, beta = to_chunks(k), to_chunks(v), to_chunks(g), to_chunks(beta)
    g = jnp.cumsum(g, axis=-1)
    idx = jnp.arange(CHUNK)
    incl = idx[:, None] >= idx[None, :]
    strict = idx[:, None] > idx[None, :]
    decay = jnp.exp(jnp.where(incl, g[..., :, None] - g[..., None, :], -jnp.inf))
    k_beta = k * beta[..., None]
    lower = jnp.where(strict, jnp.einsum('bhncd,bhnsd->bhncs', k_beta, k) * decay, 0.0)
    rhs = jnp.concatenate([v * beta[..., None], k_beta * jnp.exp(g)[..., None]], axis=-1)
    sol = lax.linalg.triangular_solve(lower, rhs, left_side=True, lower=True, unit_diagonal=True)
    u, w = sol[..., :dv], sol[..., dv:]
    attn = jnp.einsum('bhncd,bhnsd->bhncs', q, k) * decay
    q_dec = q * jnp.exp(g)[..., None]
    g_last = g[..., -1]
    k_dec = k * jnp.exp(g_last[..., None] - g)[..., None]

    xs = tuple(jnp.moveaxis(t, 2, 0) for t in (u, w, q_dec, k_dec, attn, g_last))

    def step(state, inp):
        u_i, w_i, q_i, k_i, a_i, gl_i = inp
        v_new = u_i - jnp.einsum('bhck,bhkv->bhcv', w_i, state)
        o_i = jnp.einsum('bhck,bhkv->bhcv', q_i, state) + jnp.einsum('bhcs,bhsv->bhcv', a_i, v_new)
        state = state * jnp.exp(gl_i)[..., None, None] + jnp.einsum('bhck,bhcv->bhkv', k_i, v_new)
        return state, o_i

    s0 = jnp.zeros((B, H, dk, dv), f32)
    _, o = lax.scan(step, s0, xs)
    o = jnp.moveaxis(o, 0, 2).reshape(B, H, T, dv)
    return jnp.transpose(o, (0, 2, 1, 3))


def gated_deltanet(h, w_in, conv_w, a_log, dt_bias, norm_w, w_out):
    B, T, _ = h.shape
    proj = h @ w_in
    qkv, z, b_logit, a_logit = jnp.split(
        proj, [GDN_CONV_DIM, GDN_CONV_DIM + GDN_VALUE_DIM, GDN_CONV_DIM + GDN_VALUE_DIM + GDN_V_HEADS], axis=-1)
    qkv = jax.nn.silu(causal_depthwise_conv(qkv, conv_w))
    q, k, v = jnp.split(qkv, [GDN_KEY_DIM, 2 * GDN_KEY_DIM], axis=-1)
    rep = GDN_V_HEADS // GDN_QK_HEADS
    q = jnp.repeat(l2norm(q.reshape(B, T, GDN_QK_HEADS, GDN_HEAD_DIM)), rep, axis=2)
    k = jnp.repeat(l2norm(k.reshape(B, T, GDN_QK_HEADS, GDN_HEAD_DIM)), rep, axis=2)
    v = v.reshape(B, T, GDN_V_HEADS, GDN_HEAD_DIM)
    beta = jax.nn.sigmoid(b_logit.astype(jnp.float32))
    g = -jnp.exp(a_log.astype(jnp.float32)) * jax.nn.softplus(a_logit.astype(jnp.float32) + dt_bias)
    o = chunk_gated_delta_rule(q, k, v, g, beta)
    o = o * lax.rsqrt(jnp.mean(o * o, axis=-1, keepdims=True) + RMS_EPS) * norm_w
    o = o * jax.nn.silu(z.astype(jnp.float32).reshape(B, T, GDN_V_HEADS, GDN_HEAD_DIM))
    return o.reshape(B, T, GDN_VALUE_DIM).astype(h.dtype) @ w_out


def shared_kv(x, c, w_ada_kv, b_ada_kv, w_kv):
    B, T, _ = x.shape
    shift, scale = jnp.split(ada_mod(c, w_ada_kv, b_ada_kv), 2, axis=-1)
    kv = modulate(x, shift, scale) @ w_kv
    k, v = jnp.split(kv, 2, axis=-1)
    k = jnp.transpose(k.reshape(B, T, SB_HEADS, SB_HEAD_DIM), (0, 2, 1, 3))
    v = jnp.transpose(v.reshape(B, T, SB_HEADS, SB_HEAD_DIM), (0, 2, 1, 3))
    return k, v


def stick_breaking_attention(h, k, v, w_q, w_out):
    B, T, _ = h.shape
    q = jnp.transpose((h @ w_q).reshape(B, T, SB_HEADS, SB_HEAD_DIM), (0, 2, 1, 3))
    nqb = T // SB_BLOCK
    q_blocks = jnp.moveaxis(q.reshape(B, SB_HEADS, nqb, SB_BLOCK, SB_HEAD_DIM), 2, 0)
    key_pos = jnp.arange(T)
    scale = SB_HEAD_DIM ** -0.5

    def one_block(args):
        q_blk, start = args
        z = jnp.einsum('bhqd,bhkd->bhqk', q_blk, k).astype(jnp.float32) * scale
        q_pos = start + jnp.arange(SB_BLOCK)
        mask = key_pos[None, :] < q_pos[:, None]
        log_stay = jnp.where(mask, jax.nn.log_sigmoid(-z), 0.0)
        after = lax.cumsum(log_stay, axis=3, reverse=True) - log_stay
        a = jnp.where(mask, jnp.exp(jax.nn.log_sigmoid(z) + after), 0.0)
        return jnp.einsum('bhqk,bhkd->bhqd', a.astype(v.dtype), v)

    o = lax.map(one_block, (q_blocks, jnp.arange(nqb) * SB_BLOCK))
    o = jnp.moveaxis(o, 0, 2).reshape(B, SB_HEADS, T, SB_HEAD_DIM)
    o = jnp.transpose(o, (0, 2, 1, 3)).reshape(B, T, SB_DIM)
    return o @ w_out


def swiglu(h, w_gu, w_down):
    gate, up = jnp.split(h @ w_gu, 2, axis=-1)
    return (jax.nn.silu(gate) * up) @ w_down


def moe_swiglu(h, w_router, b_router, w_gu, w_down):
    B, T, D = h.shape
    hf = h.reshape(B * T, D)
    logits = (hf @ w_router).astype(jnp.float32) + b_router
    top_logit, top_idx = lax.top_k(logits, TOP_K)
    top_w = jax.nn.softmax(top_logit, axis=-1)
    gates = jnp.sum(jax.nn.one_hot(top_idx, N_EXPERTS, dtype=jnp.float32) * top_w[..., None], axis=1)
    y = jnp.zeros_like(hf)
    for e in range(N_EXPERTS):
        y = y + gates[:, e:e + 1].astype(h.dtype) * swiglu(hf, w_gu[e], w_down[e])
    return y.reshape(B, T, D)


def setup_inputs(seed: int = 0) -> dict:
    key = jax.random.key(seed)
    ks = jax.random.split(key, 24)
    f32 = jnp.float32
    D = D_MODEL

    def nrm(k, shape, s):
        return jax.random.normal(k, shape, f32) * s

    dt = jnp.exp(jax.random.uniform(ks[9], (N_A_LAYERS, GDN_V_HEADS), f32, math.log(1e-3), math.log(1e-1)))
    return {
        'x': nrm(ks[0], (BATCH, SEQ, D), 1.0),
        'c': nrm(ks[1], (BATCH, D), 1.0),
        'w_ada': nrm(ks[2], (DEPTH, D, 6 * D), 0.5 * D ** -0.5),
        'b_ada': nrm(ks[3], (DEPTH, 6 * D), 0.02),
        'ln_g': 1.0 + nrm(ks[4], (DEPTH, 2, D), 0.02),
        'ln_b': nrm(ks[5], (DEPTH, 2, D), 0.02),
        'gdn_w_in': nrm(ks[6], (N_A_LAYERS, D, GDN_IN_DIM), D ** -0.5),
        'gdn_conv_w': nrm(ks[7], (N_A_LAYERS, GDN_CONV_DIM, CONV_WIDTH), CONV_WIDTH ** -0.5),
        'gdn_a_log': jnp.log(jax.random.uniform(ks[8], (N_A_LAYERS, GDN_V_HEADS), f32, 1.0, 16.0)),
        'gdn_dt_bias': dt + jnp.log(-jnp.expm1(-dt)),
        'gdn_norm_w': 1.0 + nrm(ks[10], (N_A_LAYERS, GDN_HEAD_DIM), 0.02),
        'gdn_w_out': nrm(ks[11], (N_A_LAYERS, GDN_VALUE_DIM, D), GDN_VALUE_DIM ** -0.5 * DN_BETA),
        'kv_w_ada': nrm(ks[12], (D, 2 * D), 0.5 * D ** -0.5),
        'kv_b_ada': nrm(ks[13], (2 * D,), 0.02),
        'sb_w_kv': nrm(ks[14], (D, 2 * SB_DIM), D ** -0.5),
        'sb_w_q': nrm(ks[15], (N_B_LAYERS, D, SB_DIM), D ** -0.5),
        'sb_w_out': nrm(ks[16], (N_B_LAYERS, SB_DIM, D), SB_DIM ** -0.5 * DN_BETA),
        'ffn_w_gu': nrm(ks[17], (N_DENSE_LAYERS, D, 2 * FFN_DIM), D ** -0.5),
        'ffn_w_down': nrm(ks[18], (N_DENSE_LAYERS, FFN_DIM, D), FFN_DIM ** -0.5 * DN_BETA),
        'moe_w_router': nrm(ks[19], (N_MOE_LAYERS, D, N_EXPERTS), D ** -0.5),
        'moe_b_router': nrm(ks[20], (N_MOE_LAYERS, N_EXPERTS), 0.01),
        'moe_w_gu': nrm(ks[21], (N_MOE_LAYERS, N_EXPERTS, D, 2 * FFN_DIM), D ** -0.5),
        'moe_w_down': nrm(ks[22], (N_MOE_LAYERS, N_EXPERTS, FFN_DIM, D), FFN_DIM ** -0.5 * DN_BETA),
    }


def reference(x, c, w_ada, b_ada, ln_g, ln_b, gdn_w_in, gdn_conv_w, gdn_a_log, gdn_dt_bias, gdn_norm_w,
              gdn_w_out, kv_w_ada, kv_b_ada, sb_w_kv, sb_w_q, sb_w_out, ffn_w_gu, ffn_w_down,
              moe_w_router, moe_b_router, moe_w_gu, moe_w_down):
    k_sh, v_sh = None, None
    for l in range(DEPTH):
        sh_m, sc_m, gt_m, sh_f, sc_f, gt_f = jnp.split(ada_mod(c, w_ada[l], b_ada[l]), 6, axis=-1)
        h = modulate(x, sh_m, sc_m)
        if l < N_A_LAYERS:
            y = gated_deltanet(h, gdn_w_in[l], gdn_conv_w[l], gdn_a_log[l], gdn_dt_bias[l],
                               gdn_norm_w[l], gdn_w_out[l])
        else:
            if l == N_A_LAYERS:
                k_sh, v_sh = shared_kv(x, c, kv_w_ada, kv_b_ada, sb_w_kv)
            j = l - N_A_LAYERS
            y = stick_breaking_attention(h, k_sh, v_sh, sb_w_q[j], sb_w_out[j])
        x = layer_norm(DN_ALPHA * x + gt_m[:, None, :] * y, ln_g[l, 0], ln_b[l, 0])
        h = modulate(x, sh_f, sc_f)
        if l % 2 == 0:
            y = swiglu(h, ffn_w_gu[l // 2], ffn_w_down[l // 2])
        else:
            y = moe_swiglu(h, moe_w_router[l // 2], moe_b_router[l // 2], moe_w_gu[l // 2], moe_w_down[l // 2])
        x = layer_norm(DN_ALPHA * x + gt_f[:, None, :] * y, ln_g[l, 1], ln_b[l, 1])
    return x
```

```python
import functools

import jax
import jax.numpy as jnp
from jax import lax
from jax.experimental import pallas as pl
from jax.experimental.pallas import tpu as pltpu

F32 = jnp.float32
BF16 = jnp.bfloat16

LANES = 128
VMEM_LIMIT_BYTES = 56 * 1024 * 1024

HEAD_DIM = 128
GDN_CHUNK = 64
CONV_WIDTH = 4
N_EXPERTS = 8
LN_EPS = 1e-5
RMS_EPS = 1e-6
L2_EPS = 1e-6
DEPTH = 2
DN_ALPHA = (2.0 * DEPTH) ** 0.25


def _cparams(*sem):
    return pltpu.CompilerParams(dimension_semantics=sem, vmem_limit_bytes=VMEM_LIMIT_BYTES)


def _dot(a, b):
    return jnp.dot(a, b, preferred_element_type=F32)


def _dot_nt(a, b):
    return lax.dot_general(a, b, (((1,), (1,)), ((), ())), preferred_element_type=F32)


def _dot_tn(a, b):
    return lax.dot_general(a, b, (((0,), (0,)), ((), ())), preferred_element_type=F32)


def _split(a):
    hi = a.astype(BF16)
    lo = (a - hi.astype(F32)).astype(BF16)
    return hi, lo


def _dot3(a, b):
    ah, al = _split(a)
    bh, bl = _split(b)
    return _dot(ah, bh) + (_dot(ah, bl) + _dot(al, bh))


def _softplus(x):
    return jnp.maximum(x, 0.0) + jnp.log(1.0 + jnp.exp(-jnp.abs(x)))


def _silu(x):
    return x * jax.nn.sigmoid(x)


def _ada_kernel(c_ref, w_ref, b_ref, o_ref):
    s = _silu(c_ref[...]).astype(BF16)
    o_ref[...] = _dot(s, w_ref[...].astype(BF16)) + b_ref[...]


def _ada(c_pad, w, layer, b, tn=1024):
    _, d, n_out = w.shape
    return pl.pallas_call(
        _ada_kernel,
        grid=(n_out // tn,),
        in_specs=[pl.BlockSpec((8, d), lambda n: (0, 0)),
                  pl.BlockSpec((None, d, tn), lambda n: (layer, 0, n)),
                  pl.BlockSpec((1, tn), lambda n: (0, n))],
        out_specs=pl.BlockSpec((8, tn), lambda n: (0, n)),
        out_shape=jax.ShapeDtypeStruct((8, n_out), F32),
        compiler_params=_cparams("arbitrary"),
    )(c_pad, w, b.reshape(1, n_out))


def _modulate_kernel(x_ref, sh_ref, sc_ref, o_ref):
    o_ref[...] = (x_ref[...] * (1.0 + sc_ref[0]) + sh_ref[0]).astype(o_ref.dtype)


def _modulate(xf, shift, scale, seq, tm=512):
    n, d = xf.shape
    bmap = lambda i: (i * tm // seq, 0, 0)
    return pl.pallas_call(
        _modulate_kernel,
        grid=(n // tm,),
        in_specs=[pl.BlockSpec((tm, d), lambda i: (i, 0)),
                  pl.BlockSpec((1, 1, d), bmap),
                  pl.BlockSpec((1, 1, d), bmap)],
        out_specs=pl.BlockSpec((tm, d), lambda i: (i, 0)),
        out_shape=jax.ShapeDtypeStruct((n, d), BF16),
        compiler_params=_cparams("arbitrary"),
    )(xf, shift, scale)


def _ws_mm_kernel(a_ref, w_ref, o_ref, wb_ref):
    @pl.when(pl.program_id(1) == 0)
    def _():
        wb_ref[...] = w_ref[...].astype(BF16)

    o_ref[...] = _dot(a_ref[...], wb_ref[...]).astype(o_ref.dtype)


def _ws_mm(a, w, layer, n_out, out_dtype, tm=512, tn=1024):
    m, k = a.shape
    tn = min(tn, n_out)
    return pl.pallas_call(
        _ws_mm_kernel,
        grid=(n_out // tn, m // tm),
        in_specs=[pl.BlockSpec((tm, k), lambda n, i: (i, 0)),
                  pl.BlockSpec((None, k, tn), lambda n, i: (layer, 0, n))],
        out_specs=pl.BlockSpec((tm, tn), lambda n, i: (i, n)),
        out_shape=jax.ShapeDtypeStruct((m, n_out), out_dtype),
        scratch_shapes=[pltpu.VMEM((k, tn), BF16)],
        compiler_params=_cparams("arbitrary", "arbitrary"),
    )(a, w)


def _ws_swiglu_kernel(a_ref, wg_ref, wu_ref, o_ref, wgb_ref, wub_ref):
    @pl.when(pl.program_id(1) == 0)
    def _():
        wgb_ref[...] = wg_ref[...].astype(BF16)
        wub_ref[...] = wu_ref[...].astype(BF16)

    a = a_ref[...]
    g = _dot(a, wgb_ref[...])
    u = _dot(a, wub_ref[...])
    o_ref[...] = (_silu(g) * u).astype(o_ref.dtype)


def _ws_swiglu(a, w_gu, layer, tm=512, tn=512):
    m, k = a.shape
    f = w_gu.shape[-1] // 2
    nt = f // tn
    return pl.pallas_call(
        _ws_swiglu_kernel,
        grid=(nt, m // tm),
        in_specs=[pl.BlockSpec((tm, k), lambda n, i: (i, 0)),
                  pl.BlockSpec((None, k, tn), lambda n, i: (layer, 0, n)),
                  pl.BlockSpec((None, k, tn), lambda n, i: (layer, 0, nt + n))],
        out_specs=pl.BlockSpec((tm, tn), lambda n, i: (i, n)),
        out_shape=jax.ShapeDtypeStruct((m, f), BF16),
        scratch_shapes=[pltpu.VMEM((k, tn), BF16), pltpu.VMEM((k, tn), BF16)],
        compiler_params=_cparams("arbitrary", "arbitrary"),
    )(a, w_gu, w_gu)


def _layer_norm(r, g, b):
    mu = jnp.mean(r, axis=-1, keepdims=True)
    rc = r - mu
    var = jnp.mean(rc * rc, axis=-1, keepdims=True)
    return rc * lax.rsqrt(var + LN_EPS) * g + b


def _proj_ln_kernel(*refs, n_mod, nk, with_router):
    a_ref, w_ref, x_ref, gate_ref, g_ref, b_ref = refs[:6]
    mod_refs = refs[6:6 + 2 * n_mod]
    pos = 6 + 2 * n_mod
    wr_ref = refs[pos] if with_router else None
    pos += int(with_router)
    xo_ref = refs[pos]
    h_refs = refs[pos + 1:pos + 1 + n_mod]
    pos += 1 + n_mod
    lg_ref = refs[pos] if with_router else None
    acc_ref = refs[-1]
    kk = pl.program_id(1)

    @pl.when(kk == 0)
    def _():
        acc_ref[...] = jnp.zeros_like(acc_ref)

    acc_ref[...] += _dot(a_ref[...], w_ref[...])

    @pl.when(kk == nk - 1)
    def _():
        r = DN_ALPHA * x_ref[...] + gate_ref[0] * acc_ref[...]
        xn = _layer_norm(r, g_ref[...], b_ref[...])
        xo_ref[...] = xn
        for t in range(n_mod):
            h = xn * (1.0 + mod_refs[2 * t + 1][0]) + mod_refs[2 * t][0]
            h_refs[t][...] = h.astype(h_refs[t].dtype)
            if with_router and t == 0:
                lg_ref[...] = _dot3(h, wr_ref[...])


def _proj_ln(a, w, xf, gate, ln_g, ln_b, mods, seq, h_dtype=BF16, w_router=None, tm=512, tk=512):
    m, k = a.shape
    d = w.shape[-1]
    nk = k // tk
    n_mod = len(mods)
    with_router = w_router is not None
    bmap = lambda i, kk: (i * tm // seq, 0, 0)
    row = lambda i, kk: (i, 0)
    const = lambda i, kk: (0, 0)
    in_specs = [pl.BlockSpec((tm, tk), lambda i, kk: (i, kk)),
                pl.BlockSpec((tk, d), lambda i, kk: (kk, 0)),
                pl.BlockSpec((tm, d), row),
                pl.BlockSpec((1, 1, d), bmap),
                pl.BlockSpec((1, d), const),
                pl.BlockSpec((1, d), const)]
    args = [a, w, xf, gate, ln_g.reshape(1, d), ln_b.reshape(1, d)]
    for sh, sc in mods:
        in_specs += [pl.BlockSpec((1, 1, d), bmap), pl.BlockSpec((1, 1, d), bmap)]
        args += [sh, sc]
    out_specs = [pl.BlockSpec((tm, d), row)] + [pl.BlockSpec((tm, d), row)] * n_mod
    out_shape = [jax.ShapeDtypeStruct((m, d), F32)] + [jax.ShapeDtypeStruct((m, d), h_dtype)] * n_mod
    if with_router:
        in_specs.append(pl.BlockSpec((d, LANES), const))
        args.append(w_router)
        out_specs.append(pl.BlockSpec((tm, LANES), row))
        out_shape.append(jax.ShapeDtypeStruct((m, LANES), F32))
    return pl.pallas_call(
        functools.partial(_proj_ln_kernel, n_mod=n_mod, nk=nk, with_router=with_router),
        grid=(m // tm, nk),
        in_specs=in_specs,
        out_specs=out_specs,
        out_shape=out_shape,
        scratch_shapes=[pltpu.VMEM((tm, d), F32)],
        compiler_params=_cparams("arbitrary", "arbitrary"),
    )(*args)


def _neumann_inverse(low):
    c = low.shape[0]
    eye = (lax.broadcasted_iota(jnp.int32, (c, c), 0) == lax.broadcasted_iota(jnp.int32, (c, c), 1)).astype(F32)
    x = eye - low
    p = _dot3(low, low)
    x = x + _dot3(x, p)
    n_terms = 4
    while n_terms < c:
        p = _dot3(p, p)
        x = x + _dot3(x, p)
        n_terms *= 2
    return x


def _gdn_kernel(q_ref, k_ref, v_ref, z_ref, cwq_ref, cwk_ref, cwv_ref, lg_ref, prm_ref, nw_ref, o_ref,
                qn_s, kn_s, gc_s, beta_s, egl_s, vb_s, wr_s, qd_s, kd_s, u_s, w_s, at_s, o_s):
    j = pl.program_id(1)
    seq = q_ref.shape[1]
    c = GDN_CHUNK
    n_chunks = seq // c
    row = lax.broadcasted_iota(jnp.int32, (seq, 1), 0)
    pos = row % c

    def conv_silu(x, cw):
        acc = x * cw[CONV_WIDTH - 1:CONV_WIDTH, :]
        for s in range(1, CONV_WIDTH):
            xs = jnp.where(row >= s, pltpu.roll(x, s, axis=0), 0.0)
            acc = acc + xs * cw[CONV_WIDTH - 1 - s:CONV_WIDTH - s, :]
        return _silu(acc)

    def l2norm(x):
        return x * lax.rsqrt(jnp.sum(x * x, axis=-1, keepdims=True) + L2_EPS)

    qn = l2norm(conv_silu(q_ref[0], cwq_ref[...])) * (HEAD_DIM ** -0.5)
    kn = l2norm(conv_silu(k_ref[0], cwk_ref[...]))
    qn_s[...] = qn
    kn_s[...] = kn
    vc = conv_silu(v_ref[0], cwv_ref[...])

    lg = lg_ref[0]
    lane = lax.broadcasted_iota(jnp.int32, lg.shape, 1)
    n_vh = 2 * pl.num_programs(1)
    g_all = prm_ref[1:2, :] * _softplus(lg + prm_ref[0:1, :])

    def select_lane(x, idx):
        return jnp.sum(jnp.where(lane == idx, x, 0.0), axis=1, keepdims=True)

    for hh in range(2):
        head = 2 * j + hh
        beta = jnp.broadcast_to(jax.nn.sigmoid(select_lane(lg, head)), (seq, HEAD_DIM))
        gc = jnp.broadcast_to(select_lane(g_all, n_vh + head), (seq, HEAD_DIM))
        s = 1
        while s < c:
            gc = gc + jnp.where(pos >= s, pltpu.roll(gc, s, axis=0), 0.0)
            s *= 2
        gl = jnp.broadcast_to(gc.reshape(n_chunks, c, HEAD_DIM)[:, c - 1:c, :],
                              (n_chunks, c, HEAD_DIM)).reshape(seq, HEAD_DIM)
        eg = jnp.exp(gc)
        kb = kn * beta
        gc_s[hh] = gc
        beta_s[hh] = beta
        egl_s[hh] = jnp.exp(gl)
        vb_s[hh] = vc[:, hh * HEAD_DIM:(hh + 1) * HEAD_DIM] * beta
        wr_s[hh] = kb * eg
        qd_s[hh] = (qn * eg).astype(BF16)
        kd_s[hh] = (kn * jnp.exp(gl - gc)).astype(BF16)

    ri = lax.broadcasted_iota(jnp.int32, (c, c), 0)
    ci = lax.broadcasted_iota(jnp.int32, (c, c), 1)
    incl = ri >= ci
    strict = ri > ci
    eye = ri == ci
    ones_cc = jnp.ones((c, c), BF16)

    def chunk_prepare(ch, carry):
        r0 = pl.multiple_of(ch * c, c)
        rows = pl.ds(r0, c)
        kc = kn_s[rows, :].astype(BF16)
        qc = qn_s[rows, :].astype(BF16)
        kk = _dot_nt(kc, kc)
        qk = _dot_nt(qc, kc)
        for hh in range(2):
            col_b = gc_s[hh, rows, :][:, :c]
            dh, dl = _split(jnp.where(eye, col_b, 0.0))
            dl2 = (jnp.where(eye, col_b, 0.0) - dh.astype(F32) - dl.astype(F32)).astype(BF16)
            row_b = _dot(ones_cc, dh) + (_dot(ones_cc, dl) + _dot(ones_cc, dl2))
            dec = jnp.exp(jnp.where(incl, col_b - row_b, -jnp.inf))
            low = jnp.where(strict, kk * beta_s[hh, rows, :][:, :c] * dec, 0.0)
            tinv = _neumann_inverse(low)
            rhs = jnp.concatenate([vb_s[hh, rows, :], wr_s[hh, rows, :]], axis=1)
            sol = _dot3(tinv, rhs)
            u_s[hh, rows, :] = sol[:, :HEAD_DIM]
            w_s[hh, rows, :] = sol[:, HEAD_DIM:].astype(BF16)
            at_s[hh, rows, :] = (qk * dec).astype(BF16)
        return carry

    lax.fori_loop(0, n_chunks, chunk_prepare, 0)

    def chunk_scan(ch, states):
        r0 = pl.multiple_of(ch * c, c)
        rows = pl.ds(r0, c)
        new_states = []
        for hh in range(2):
            st = states[hh]
            sb = st.astype(BF16)
            v_new = u_s[hh, rows, :] - _dot(w_s[hh, rows, :], sb)
            vb16 = v_new.astype(BF16)
            o_s[hh, rows, :] = _dot(qd_s[hh, rows, :], sb) + _dot(at_s[hh, rows, :], vb16)
            decay = jnp.broadcast_to(egl_s[hh, pl.ds(r0, 1), :], (HEAD_DIM, HEAD_DIM))
            new_states.append(st * decay + _dot_tn(kd_s[hh, rows, :], vb16))
        return tuple(new_states)

    zero = jnp.zeros((HEAD_DIM, HEAD_DIM), F32)
    lax.fori_loop(0, n_chunks, chunk_scan, (zero, zero))

    for hh in range(2):
        o = o_s[hh]
        o = o * lax.rsqrt(jnp.mean(o * o, axis=-1, keepdims=True) + RMS_EPS) * nw_ref[...]
        zg = z_ref[0, :, hh * HEAD_DIM:(hh + 1) * HEAD_DIM]
        o_ref[0, :, hh * HEAD_DIM:(hh + 1) * HEAD_DIM] = (o * _silu(zg)).astype(o_ref.dtype)


def _gdn(proj, lg, conv_wt, prm, norm_w, n_qk_heads):
    bsz, seq, _ = proj.shape
    hq = n_qk_heads
    hd = HEAD_DIM
    f32_s = lambda: pltpu.VMEM((2, seq, hd), F32)
    bf_s = lambda: pltpu.VMEM((2, seq, hd), BF16)
    return pl.pallas_call(
        _gdn_kernel,
        grid=(bsz, hq),
        in_specs=[pl.BlockSpec((1, seq, hd), lambda b, j: (b, 0, j)),
                  pl.BlockSpec((1, seq, hd), lambda b, j: (b, 0, hq + j)),
                  pl.BlockSpec((1, seq, 2 * hd), lambda b, j: (b, 0, hq + j)),
                  pl.BlockSpec((1, seq, 2 * hd), lambda b, j: (b, 0, 2 * hq + j)),
                  pl.BlockSpec((CONV_WIDTH, hd), lambda b, j: (0, j)),
                  pl.BlockSpec((CONV_WIDTH, hd), lambda b, j: (0, hq + j)),
                  pl.BlockSpec((CONV_WIDTH, 2 * hd), lambda b, j: (0, hq + j)),
                  pl.BlockSpec((1, seq, LANES), lambda b, j: (b, 0, 0)),
                  pl.BlockSpec((8, LANES), lambda b, j: (0, 0)),
                  pl.BlockSpec((1, hd), lambda b, j: (0, 0))],
        out_specs=pl.BlockSpec((1, seq, 2 * hd), lambda b, j: (b, 0, j)),
        out_shape=jax.ShapeDtypeStruct((bsz, seq, 2 * hq * hd), BF16),
        scratch_shapes=[pltpu.VMEM((seq, hd), F32), pltpu.VMEM((seq, hd), F32),
                        f32_s(), f32_s(), f32_s(),
                        f32_s(), f32_s(),
                        bf_s(), bf_s(),
                        f32_s(), bf_s(),
                        pltpu.VMEM((2, seq, GDN_CHUNK), BF16),
                        f32_s()],
        compiler_params=_cparams("arbitrary", "arbitrary"),
    )(proj, proj, proj, proj, conv_wt, conv_wt, conv_wt, lg, prm, norm_w)


def _sb_kernel(q_ref, k_ref, v_ref, o_ref, *, blk):
    i = pl.program_id(2)
    q = q_ref[0]
    scale = HEAD_DIM ** -0.5
    tri = (lax.broadcasted_iota(jnp.int32, (blk, blk), 0) >= lax.broadcasted_iota(jnp.int32, (blk, blk), 1)).astype(BF16)
    q_pos = i * blk + lax.broadcasted_iota(jnp.int32, (blk, 1), 0)
    k_off = lax.broadcasted_iota(jnp.int32, (1, blk), 1)

    def body(step, carry):
        acc, run = carry
        jb = i - step
        rows = pl.ds(pl.multiple_of(jb * blk, blk), blk)
        z = _dot_nt(q, k_ref[0, rows, :]) * scale
        mask = (jb * blk + k_off) < q_pos
        ls = jnp.where(mask, -_softplus(z), 0.0)
        hi, lo = _split(ls)
        rc = _dot(hi, tri) + _dot(lo, tri)
        a = jnp.where(mask, jnp.exp(z + rc + run), 0.0)
        acc = acc + _dot(a.astype(BF16), v_ref[0, rows, :])
        return acc, run + rc[:, :1]

    acc, _ = lax.fori_loop(0, i + 1, body, (jnp.zeros((blk, HEAD_DIM), F32), jnp.zeros((blk, 1), F32)))
    o_ref[0] = acc.astype(o_ref.dtype)


def _sb_attention(q, kv, n_heads, blk=256):
    bsz, seq, _ = q.shape
    hd = HEAD_DIM
    return pl.pallas_call(
        functools.partial(_sb_kernel, blk=blk),
        grid=(bsz, n_heads, seq // blk),
        in_specs=[pl.BlockSpec((1, blk, hd), lambda b, h, i: (b, i, h)),
                  pl.BlockSpec((1, seq, hd), lambda b, h, i: (b, 0, h)),
                  pl.BlockSpec((1, seq, hd), lambda b, h, i: (b, 0, n_heads + h))],
        out_specs=pl.BlockSpec((1, blk, hd), lambda b, h, i: (b, i, h)),
        out_shape=jax.ShapeDtypeStruct((bsz, seq, n_heads * hd), BF16),
        compiler_params=_cparams("arbitrary", "arbitrary", "arbitrary"),
    )(q, kv, kv)


def _route_kernel(lg_ref, br_ref, idx_ref, wt_ref, cnt_ref, carry_ref, *, tm):
    i = pl.program_id(0)

    @pl.when(i == 0)
    def _():
        carry_ref[...] = jnp.zeros_like(carry_ref)

    lane = lax.broadcasted_iota(jnp.int32, (tm, LANES), 1)
    lane_f = lane.astype(F32)
    logits = jnp.where(lane < N_EXPERTS, lg_ref[...] + br_ref[...], -jnp.inf)
    m1 = jnp.max(logits, axis=1, keepdims=True)
    i1 = jnp.min(jnp.where(logits == m1, lane_f, float(LANES)), axis=1, keepdims=True)
    oh1 = lane_f == i1
    rest = jnp.where(oh1, -jnp.inf, logits)
    m2 = jnp.max(rest, axis=1, keepdims=True)
    i2 = jnp.min(jnp.where(rest == m2, lane_f, float(LANES)), axis=1, keepdims=True)
    oh2 = lane_f == i2
    e21 = jnp.exp(m2 - m1)
    w1 = 1.0 / (1.0 + e21)
    w2 = e21 / (1.0 + e21)

    cnt = jnp.where(oh1, 1.0, jnp.where(oh2, 1.0, 0.0))
    before = (lax.broadcasted_iota(jnp.int32, (tm, tm), 1) < lax.broadcasted_iota(jnp.int32, (tm, tm), 0)).astype(BF16)
    rank = _dot(before, cnt.astype(BF16)) + carry_ref[0:1, :]
    r1 = jnp.sum(jnp.where(oh1, rank, 0.0), axis=1, keepdims=True)
    r2 = jnp.sum(jnp.where(oh2, rank, 0.0), axis=1, keepdims=True)
    total = carry_ref[0:1, :] + jnp.sum(cnt, axis=0, keepdims=True)
    carry_ref[...] = jnp.broadcast_to(total, carry_ref.shape)
    cnt_ref[...] = jnp.broadcast_to(total, cnt_ref.shape)

    packed = jnp.where(lane == 0, i1, jnp.where(lane == 1, i2, jnp.where(lane == 2, r1, jnp.where(lane == 3, r2, 0.0))))
    idx_ref[...] = packed.astype(jnp.int32)
    wt_ref[...] = jnp.where(lane == 0, w1, jnp.where(lane == 1, w2, 0.0))


def _route(logits, b_router_pad, tm=512):
    n = logits.shape[0]
    row = lambda i: (i, 0)
    return pl.pallas_call(
        functools.partial(_route_kernel, tm=tm),
        grid=(n // tm,),
        in_specs=[pl.BlockSpec((tm, LANES), row), pl.BlockSpec((1, LANES), lambda i: (0, 0))],
        out_specs=[pl.BlockSpec((tm, LANES), row), pl.BlockSpec((tm, LANES), row),
                   pl.BlockSpec((8, LANES), lambda i: (0, 0))],
        out_shape=[jax.ShapeDtypeStruct((n, LANES), jnp.int32), jax.ShapeDtypeStruct((n, LANES), F32),
                   jax.ShapeDtypeStruct((8, LANES), F32)],
        scratch_shapes=[pltpu.VMEM((8, LANES), F32)],
        compiler_params=_cparams("arbitrary"),
    )(logits, b_router_pad)


def _row_copy(src_hbm, src_row, dst, dst_row, sem):
    return pltpu.make_async_copy(src_hbm.at[pl.ds(src_row, 1)], dst.at[pl.ds(dst_row, 1)], sem)


def _gather_kernel(src_ref, h_hbm, o_ref, buf, sem, *, tg):
    base = pl.program_id(0) * tg

    def start(r, carry):
        _row_copy(h_hbm, src_ref[base + r], buf, r, sem).start()
        return carry

    def wait(r, carry):
        _row_copy(h_hbm, 0, buf, r, sem).wait()
        return carry

    lax.fori_loop(0, tg, start, 0)
    lax.fori_loop(0, tg, wait, 0)
    o_ref[...] = buf[...].astype(o_ref.dtype)


def _gather_rows(h, src, tg=256):
    m_pad = src.shape[0]
    d = h.shape[1]
    return pl.pallas_call(
        functools.partial(_gather_kernel, tg=tg),
        grid_spec=pltpu.PrefetchScalarGridSpec(
            num_scalar_prefetch=1,
            grid=(m_pad // tg,),
            in_specs=[pl.BlockSpec(memory_space=pl.ANY)],
            out_specs=pl.BlockSpec((tg, d), lambda i, src: (i, 0)),
            scratch_shapes=[pltpu.VMEM((tg, d), h.dtype), pltpu.SemaphoreType.DMA(())]),
        out_shape=jax.ShapeDtypeStruct((m_pad, d), BF16),
        compiler_params=_cparams("arbitrary"),
    )(src, h)


def _group_swiglu_kernel(te_ref, tv_ref, a_ref, wg_ref, wu_ref, o_ref, wgb_ref, wub_ref):
    i = pl.program_id(1)
    new_expert = jnp.logical_or(i == 0, te_ref[i] != te_ref[jnp.maximum(i - 1, 0)])

    @pl.when(new_expert)
    def _():
        wgb_ref[...] = wg_ref[...].astype(BF16)
        wub_ref[...] = wu_ref[...].astype(BF16)

    @pl.when(tv_ref[i] > 0)
    def _():
        a = a_ref[...]
        g = _dot(a, wgb_ref[...])
        u = _dot(a, wub_ref[...])
        o_ref[...] = (_silu(g) * u).astype(o_ref.dtype)

    @pl.when(tv_ref[i] == 0)
    def _():
        o_ref[...] = jnp.zeros_like(o_ref)


def _group_swiglu(xs, w_gu, layer, tile_expert, tile_valid, tm, tn=512):
    m_pad, k = xs.shape
    f = w_gu.shape[-1] // 2
    nt = f // tn
    return pl.pallas_call(
        _group_swiglu_kernel,
        grid_spec=pltpu.PrefetchScalarGridSpec(
            num_scalar_prefetch=2,
            grid=(nt, m_pad // tm),
            in_specs=[pl.BlockSpec((tm, k), lambda n, i, te, tv: (i, 0)),
                      pl.BlockSpec((None, None, k, tn), lambda n, i, te, tv: (layer, te[i], 0, n)),
                      pl.BlockSpec((None, None, k, tn), lambda n, i, te, tv: (layer, te[i], 0, nt + n))],
            out_specs=pl.BlockSpec((tm, tn), lambda n, i, te, tv: (i, n)),
            scratch_shapes=[pltpu.VMEM((k, tn), BF16), pltpu.VMEM((k, tn), BF16)]),
        out_shape=jax.ShapeDtypeStruct((m_pad, f), BF16),
        compiler_params=_cparams("arbitrary", "arbitrary"),
    )(tile_expert, tile_valid, xs, w_gu, w_gu)


def _group_down_kernel(te_ref, tv_ref, a_ref, w_ref, o_ref, *, nk):
    i = pl.program_id(0)
    kk = pl.program_id(1)

    @pl.when(kk == 0)
    def _():
        o_ref[...] = jnp.zeros_like(o_ref)

    @pl.when(tv_ref[i] > 0)
    def _():
        o_ref[...] += _dot(a_ref[...], w_ref[...])


def _group_down(hm, w_down, tile_expert, tile_valid, tm, tk=512):
    m_pad, f = hm.shape
    d = w_down.shape[-1]
    nk = f // tk
    return pl.pallas_call(
        functools.partial(_group_down_kernel, nk=nk),
        grid_spec=pltpu.PrefetchScalarGridSpec(
            num_scalar_prefetch=2,
            grid=(m_pad // tm, nk),
            in_specs=[pl.BlockSpec((tm, tk), lambda i, kk, te, tv: (i, kk)),
                      pl.BlockSpec((None, tk, d), lambda i, kk, te, tv: (te[i], kk, 0))],
            out_specs=pl.BlockSpec((tm, d), lambda i, kk, te, tv: (i, 0))),
        out_shape=jax.ShapeDtypeStruct((m_pad, d), F32),
        compiler_params=_cparams("arbitrary", "arbitrary"),
    )(tile_expert, tile_valid, hm, w_down)


def _combine_ln_kernel(d1_ref, d2_ref, o_hbm, wt_ref, x_ref, gate_ref, g_ref, b_ref, xo_ref, buf1, buf2, sem, *, tm):
    base = pl.program_id(0) * tm

    def start(r, carry):
        _row_copy(o_hbm, d1_ref[base + r], buf1, r, sem).start()
        _row_copy(o_hbm, d2_ref[base + r], buf2, r, sem).start()
        return carry

    def wait(r, carry):
        _row_copy(o_hbm, 0, buf1, r, sem).wait()
        _row_copy(o_hbm, 0, buf2, r, sem).wait()
        return carry

    lax.fori_loop(0, tm, start, 0)
    lax.fori_loop(0, tm, wait, 0)
    wt = wt_ref[...]
    y = wt[:, 0:1] * buf1[...] + wt[:, 1:2] * buf2[...]
    r = DN_ALPHA * x_ref[...] + gate_ref[0] * y
    xo_ref[...] = _layer_norm(r, g_ref[...], b_ref[...])


def _combine_ln(o_sorted, d1, d2, wts, xf, gate, ln_g, ln_b, seq, tm=256):
    n, d = xf.shape
    row = lambda i, d1, d2: (i, 0)
    const = lambda i, d1, d2: (0, 0)
    return pl.pallas_call(
        functools.partial(_combine_ln_kernel, tm=tm),
        grid_spec=pltpu.PrefetchScalarGridSpec(
            num_scalar_prefetch=2,
            grid=(n // tm,),
            in_specs=[pl.BlockSpec(memory_space=pl.ANY),
                      pl.BlockSpec((tm, LANES), row),
                      pl.BlockSpec((tm, d), row),
                      pl.BlockSpec((1, 1, d), lambda i, d1, d2: (i * tm // seq, 0, 0)),
                      pl.BlockSpec((1, d), const),
                      pl.BlockSpec((1, d), const)],
            out_specs=pl.BlockSpec((tm, d), row),
            scratch_shapes=[pltpu.VMEM((tm, d), F32), pltpu.VMEM((tm, d), F32), pltpu.SemaphoreType.DMA(())]),
        out_shape=jax.ShapeDtypeStruct((n, d), F32),
        compiler_params=_cparams("arbitrary"),
    )(d1, d2, o_sorted, wts, xf, gate, ln_g.reshape(1, d), ln_b.reshape(1, d))


def kernel(x, c, w_ada, b_ada, ln_g, ln_b, gdn_w_in, gdn_conv_w, gdn_a_log, gdn_dt_bias, gdn_norm_w, gdn_w_out,
           kv_w_ada, kv_b_ada, sb_w_kv, sb_w_q, sb_w_out, ffn_w_gu, ffn_w_down, moe_w_router, moe_b_router,
           moe_w_gu, moe_w_down):
    bsz, seq, d = x.shape
    n = bsz * seq
    assert w_ada.shape[0] == DEPTH and gdn_w_in.shape[0] == 1 and sb_w_q.shape[0] == 1
    n_vh = gdn_a_log.shape[1]
    n_qk = n_vh // 2
    key_dim = n_qk * HEAD_DIM
    value_dim = n_vh * HEAD_DIM
    main_cols = 2 * key_dim + 2 * value_dim
    n_sb_heads = sb_w_q.shape[-1] // HEAD_DIM
    xf = x.reshape(n, d)

    c_pad = jnp.zeros((8, d), F32).at[:bsz].set(c)
    def mod_vectors(w, layer, b, parts):
        m = _ada(c_pad, w, layer, b)[:bsz]
        return [v.reshape(bsz, 1, d) for v in jnp.split(m, parts, axis=-1)]
    sh_m0, sc_m0, gt_m0, sh_f0, sc_f0, gt_f0 = mod_vectors(w_ada, 0, b_ada[0], 6)
    sh_m1, sc_m1, gt_m1, sh_f1, sc_f1, gt_f1 = mod_vectors(w_ada, 1, b_ada[1], 6)
    sh_kv, sc_kv = mod_vectors(kv_w_ada[None], 0, kv_b_ada, 2)

    h = _modulate(xf, sh_m0, sc_m0, seq)
    proj = _ws_mm(h, gdn_w_in, 0, main_cols, F32)
    w_gate = jnp.zeros((1, d, LANES), F32).at[0, :, :2 * n_vh].set(gdn_w_in[0, :, main_cols:])
    gate_logits = _ws_mm(h, w_gate, 0, LANES, F32)
    prm = jnp.zeros((8, LANES), F32)
    prm = prm.at[0, n_vh:2 * n_vh].set(gdn_dt_bias[0].astype(F32))
    prm = prm.at[1, n_vh:2 * n_vh].set(-jnp.exp(gdn_a_log[0].astype(F32)))
    conv_wt = jnp.transpose(gdn_conv_w[0])
    o = _gdn(proj.reshape(bsz, seq, main_cols), gate_logits.reshape(bsz, seq, LANES), conv_wt, prm,
             gdn_norm_w[0].reshape(1, HEAD_DIM), n_qk)
    x1, h1 = _proj_ln(o.reshape(n, value_dim), gdn_w_out[0].astype(BF16), xf, gt_m0, ln_g[0, 0], ln_b[0, 0],
                      [(sh_f0, sc_f0)], seq)

    hm = _ws_swiglu(h1, ffn_w_gu, 0)
    x2, h2, h_kv = _proj_ln(hm, ffn_w_down[0].astype(BF16), x1, gt_f0, ln_g[0, 1], ln_b[0, 1],
                            [(sh_m1, sc_m1), (sh_kv, sc_kv)], seq)

    kv = _ws_mm(h_kv, sb_w_kv[None], 0, 2 * n_sb_heads * HEAD_DIM, BF16)
    q = _ws_mm(h2, sb_w_q, 0, n_sb_heads * HEAD_DIM, BF16)
    oa = _sb_attention(q.reshape(bsz, seq, -1), kv.reshape(bsz, seq, -1), n_sb_heads)
    w_router = jnp.zeros((d, LANES), F32).at[:, :N_EXPERTS].set(moe_w_router[0])
    x3, h3, logits = _proj_ln(oa.reshape(n, -1), sb_w_out[0].astype(BF16), x2, gt_m1, ln_g[1, 0], ln_b[1, 0],
                              [(sh_f1, sc_f1)], seq, h_dtype=F32, w_router=w_router)

    b_router = jnp.zeros((1, LANES), F32).at[0, :N_EXPERTS].set(moe_b_router[0].astype(F32))
    idx, wts, cnt = _route(logits, b_router)
    tm_g = 512
    n_tiles = 2 * n // tm_g + N_EXPERTS
    counts = cnt[0, :N_EXPERTS].astype(jnp.int32)
    tiles_per = (counts + tm_g - 1) // tm_g
    tile_end = jnp.cumsum(tiles_per)
    row_off = (tile_end - tiles_per) * tm_g
    tile_ids = jnp.arange(n_tiles, dtype=jnp.int32)
    tile_expert = jnp.minimum(jnp.searchsorted(tile_end, tile_ids, side="right"), N_EXPERTS - 1).astype(jnp.int32)
    tile_valid = (tile_ids < tile_end[-1]).astype(jnp.int32)
    dest1 = row_off[idx[:, 0]] + idx[:, 2]
    dest2 = row_off[idx[:, 1]] + idx[:, 3]
    tok = jnp.arange(n, dtype=jnp.int32)
    src = jnp.zeros((n_tiles * tm_g,), jnp.int32).at[dest1].set(tok).at[dest2].set(tok)

    xs = _gather_rows(h3, src)
    hm2 = _group_swiglu(xs, moe_w_gu, 0, tile_expert, tile_valid, tm_g)
    o_sorted = _group_down(hm2, moe_w_down[0].astype(BF16), tile_expert, tile_valid, tm_g)
    out = _combine_ln(o_sorted, dest1, dest2, wts, x3, gt_f1, ln_g[1, 1], ln_b[1, 1], seq)
    return out.reshape(bsz, seq, d)
```

```python
import functools

import jax
import jax.numpy as jnp
from jax import lax
from jax.experimental import pallas as pl
from jax.experimental.pallas import tpu as pltpu

F32 = jnp.float32
BF16 = jnp.bfloat16

LANES = 128
VMEM_LIMIT_BYTES = 56 * 1024 * 1024

HEAD_DIM = 128
GDN_CHUNK = 64
CONV_WIDTH = 4
N_EXPERTS = 8
LN_EPS = 1e-5
RMS_EPS = 1e-6
L2_EPS = 1e-6
DEPTH = 2
DN_ALPHA = (2.0 * DEPTH) ** 0.25


def _cparams(*sem):
    return pltpu.CompilerParams(dimension_semantics=sem, vmem_limit_bytes=VMEM_LIMIT_BYTES)


def _dot(a, b):
    return jnp.dot(a, b, preferred_element_type=F32)


def _dot_nt(a, b):
    return lax.dot_general(a, b, (((1,), (1,)), ((), ())), preferred_element_type=F32)


def _dot_tn(a, b):
    return lax.dot_general(a, b, (((0,), (0,)), ((), ())), preferred_element_type=F32)


def _split(a):
    hi = a.astype(BF16)
    lo = (a - hi.astype(F32)).astype(BF16)
    return hi, lo


def _dot3(a, b):
    ah, al = _split(a)
    bh, bl = _split(b)
    return _dot(ah, bh) + (_dot(ah, bl) + _dot(al, bh))


def _softplus(x):
    return jnp.maximum(x, 0.0) + jnp.log(1.0 + jnp.exp(-jnp.abs(x)))


def _silu(x):
    return x * jax.nn.sigmoid(x)


def _ada_kernel(c_ref, w_ref, b_ref, o_ref):
    s = _silu(c_ref[...]).astype(BF16)
    o_ref[...] = _dot(s, w_ref[...].astype(BF16)) + b_ref[...]


def _ada(c_pad, w, layer, b, tn=1024):
    _, d, n_out = w.shape
    return pl.pallas_call(
        _ada_kernel,
        grid=(n_out // tn,),
        in_specs=[pl.BlockSpec((8, d), lambda n: (0, 0)),
                  pl.BlockSpec((None, d, tn), lambda n: (layer, 0, n)),
                  pl.BlockSpec((1, tn), lambda n: (0, n))],
        out_specs=pl.BlockSpec((8, tn), lambda n: (0, n)),
        out_shape=jax.ShapeDtypeStruct((8, n_out), F32),
        compiler_params=_cparams("arbitrary"),
    )(c_pad, w, b.reshape(1, n_out))


def _modulate_kernel(x_ref, sh_ref, sc_ref, o_ref):
    o_ref[...] = (x_ref[...] * (1.0 + sc_ref[0]) + sh_ref[0]).astype(o_ref.dtype)


def _modulate(xf, shift, scale, seq, tm=512):
    n, d = xf.shape
    bmap = lambda i: (i * tm // seq, 0, 0)
    return pl.pallas_call(
        _modulate_kernel,
        grid=(n // tm,),
        in_specs=[pl.BlockSpec((tm, d), lambda i: (i, 0)),
                  pl.BlockSpec((1, 1, d), bmap),
                  pl.BlockSpec((1, 1, d), bmap)],
        out_specs=pl.BlockSpec((tm, d), lambda i: (i, 0)),
        out_shape=jax.ShapeDtypeStruct((n, d), BF16),
        compiler_params=_cparams("arbitrary"),
    )(xf, shift, scale)


def _ws_mm_kernel(a_ref, w_ref, o_ref, wb_ref):
    @pl.when(pl.program_id(1) == 0)
    def _():
        wb_ref[...] = w_ref[...].astype(BF16)

    o_ref[...] = _dot(a_ref[...], wb_ref[...]).astype(o_ref.dtype)


def _ws_mm(a, w, layer, n_out, out_dtype, tm=1024, tn=1024):
    m, k = a.shape
    tn = min(tn, n_out)
    return pl.pallas_call(
        _ws_mm_kernel,
        grid=(n_out // tn, m // tm),
        in_specs=[pl.BlockSpec((tm, k), lambda n, i: (i, 0)),
                  pl.BlockSpec((None, k, tn), lambda n, i: (layer, 0, n))],
        out_specs=pl.BlockSpec((tm, tn), lambda n, i: (i, n)),
        out_shape=jax.ShapeDtypeStruct((m, n_out), out_dtype),
        scratch_shapes=[pltpu.VMEM((k, tn), BF16)],
        compiler_params=_cparams("arbitrary", "arbitrary"),
    )(a, w)


def _ws_swiglu_kernel(a_ref, wg_ref, wu_ref, o_ref, wgb_ref, wub_ref):
    @pl.when(pl.program_id(1) == 0)
    def _():
        wgb_ref[...] = wg_ref[...].astype(BF16)
        wub_ref[...] = wu_ref[...].astype(BF16)

    a = a_ref[...]
    g = _dot(a, wgb_ref[...])
    u = _dot(a, wub_ref[...])
    o_ref[...] = (_silu(g) * u).astype(o_ref.dtype)


def _ws_swiglu(a, w_gu, layer, tm=1024, tn=512):
    m, k = a.shape
    f = w_gu.shape[-1] // 2
    nt = f // tn
    return pl.pallas_call(
        _ws_swiglu_kernel,
        grid=(nt, m // tm),
        in_specs=[pl.BlockSpec((tm, k), lambda n, i: (i, 0)),
                  pl.BlockSpec((None, k, tn), lambda n, i: (layer, 0, n)),
                  pl.BlockSpec((None, k, tn), lambda n, i: (layer, 0, nt + n))],
        out_specs=pl.BlockSpec((tm, tn), lambda n, i: (i, n)),
        out_shape=jax.ShapeDtypeStruct((m, f), BF16),
        scratch_shapes=[pltpu.VMEM((k, tn), BF16), pltpu.VMEM((k, tn), BF16)],
        compiler_params=_cparams("arbitrary", "arbitrary"),
    )(a, w_gu, w_gu)


def _layer_norm(r, g, b):
    mu = jnp.mean(r, axis=-1, keepdims=True)
    rc = r - mu
    var = jnp.mean(rc * rc, axis=-1, keepdims=True)
    return rc * lax.rsqrt(var + LN_EPS) * g + b


def _proj_ln_kernel(*refs, n_mod, nk, with_router):
    a_ref, w_ref, x_ref, gate_ref, g_ref, b_ref = refs[:6]
    mod_refs = refs[6:6 + 2 * n_mod]
    pos = 6 + 2 * n_mod
    wr_ref = refs[pos] if with_router else None
    pos += int(with_router)
    xo_ref = refs[pos]
    h_refs = refs[pos + 1:pos + 1 + n_mod]
    pos += 1 + n_mod
    lg_ref = refs[pos] if with_router else None
    acc_ref = refs[-1]
    kk = pl.program_id(1)

    @pl.when(kk == 0)
    def _():
        acc_ref[...] = jnp.zeros_like(acc_ref)

    acc_ref[...] += _dot(a_ref[...], w_ref[...])

    @pl.when(kk == nk - 1)
    def _():
        r = DN_ALPHA * x_ref[...] + gate_ref[0] * acc_ref[...]
        xn = _layer_norm(r, g_ref[...], b_ref[...])
        xo_ref[...] = xn
        for t in range(n_mod):
            h = xn * (1.0 + mod_refs[2 * t + 1][0]) + mod_refs[2 * t][0]
            h_refs[t][...] = h.astype(h_refs[t].dtype)
            if with_router and t == 0:
                lg_ref[...] = _dot3(h, wr_ref[...])


def _proj_ln(a, w, xf, gate, ln_g, ln_b, mods, seq, h_dtype=BF16, w_router=None, tm=512, tk=1024):
    m, k = a.shape
    d = w.shape[-1]
    nk = k // tk
    n_mod = len(mods)
    with_router = w_router is not None
    bmap = lambda i, kk: (i * tm // seq, 0, 0)
    row = lambda i, kk: (i, 0)
    const = lambda i, kk: (0, 0)
    in_specs = [pl.BlockSpec((tm, tk), lambda i, kk: (i, kk)),
                pl.BlockSpec((tk, d), lambda i, kk: (kk, 0)),
                pl.BlockSpec((tm, d), row),
                pl.BlockSpec((1, 1, d), bmap),
                pl.BlockSpec((1, d), const),
                pl.BlockSpec((1, d), const)]
    args = [a, w, xf, gate, ln_g.reshape(1, d), ln_b.reshape(1, d)]
    for sh, sc in mods:
        in_specs += [pl.BlockSpec((1, 1, d), bmap), pl.BlockSpec((1, 1, d), bmap)]
        args += [sh, sc]
    out_specs = [pl.BlockSpec((tm, d), row)] + [pl.BlockSpec((tm, d), row)] * n_mod
    out_shape = [jax.ShapeDtypeStruct((m, d), F32)] + [jax.ShapeDtypeStruct((m, d), h_dtype)] * n_mod
    if with_router:
        in_specs.append(pl.BlockSpec((d, LANES), const))
        args.append(w_router)
        out_specs.append(pl.BlockSpec((tm, LANES), row))
        out_shape.append(jax.ShapeDtypeStruct((m, LANES), F32))
    return pl.pallas_call(
        functools.partial(_proj_ln_kernel, n_mod=n_mod, nk=nk, with_router=with_router),
        grid=(m // tm, nk),
        in_specs=in_specs,
        out_specs=out_specs,
        out_shape=out_shape,
        scratch_shapes=[pltpu.VMEM((tm, d), F32)],
        compiler_params=_cparams("arbitrary", "arbitrary"),
    )(*args)


GDN_GROUP = 4


def _gdn_kernel(q_ref, k_ref, v_ref, z_ref, cwq_ref, cwk_ref, cwv_ref, lg_ref, lgt_ref, prm_ref, prow_ref, nw_ref,
                o_ref, qn_s, kn_s, vc_s, ga_s, gcr_s, kw_r, nn_r, qp_r, op_r, dg_r):
    j = pl.program_id(1)
    seq = q_ref.shape[1]
    c = GDN_CHUNK
    hd = HEAD_DIM
    n_groups = seq // (c * GDN_GROUP)
    n_vh = 2 * pl.num_programs(1)
    row = lax.broadcasted_iota(jnp.int32, (seq, 1), 0)

    def conv_silu(x, cw):
        acc = x * cw[CONV_WIDTH - 1:CONV_WIDTH, :]
        for s in range(1, CONV_WIDTH):
            xs = jnp.where(row >= s, pltpu.roll(x, s, axis=0), 0.0)
            acc = acc + xs * cw[CONV_WIDTH - 1 - s:CONV_WIDTH - s, :]
        return _silu(acc)

    def l2norm(x):
        return x * lax.rsqrt(jnp.sum(x * x, axis=-1, keepdims=True) + L2_EPS)

    qn_s[...] = l2norm(conv_silu(q_ref[0], cwq_ref[...])) * (hd ** -0.5)
    kn_s[...] = l2norm(conv_silu(k_ref[0], cwk_ref[...]))
    vc_s[...] = conv_silu(v_ref[0], cwv_ref[...])
    ga_s[...] = prm_ref[1:2, :] * _softplus(lg_ref[0] + prm_ref[0:1, :])

    ri = lax.broadcasted_iota(jnp.int32, (c, c), 0)
    ci = lax.broadcasted_iota(jnp.int32, (c, c), 1)
    upper = (ri <= ci).astype(BF16)
    for hh in range(2):
        g_row = prow_ref[hh, 1:2, :] * _softplus(lgt_ref[0, hh] + prow_ref[hh, 0:1, :])
        g1, g2 = _split(g_row)
        g3 = (g_row - g1.astype(F32) - g2.astype(F32)).astype(BF16)
        gcr_s[hh] = _dot(g1, upper) + (_dot(g2, upper) + _dot(g3, upper))

    incl = ri >= ci
    strict = ri > ci
    lane = lax.broadcasted_iota(jnp.int32, (c, LANES), 1)
    pos = lax.broadcasted_iota(jnp.int32, (c, 1), 0)

    def select_lane(x, idx):
        return jnp.sum(jnp.where(lane == idx, x, 0.0), axis=1, keepdims=True)

    def prepare_group(first_chunk, slot):
        chunks = []
        for k in range(GDN_GROUP):
            rows = pl.ds(pl.multiple_of((first_chunk + k) * c, c), c)
            kc = kn_s[rows, :]
            qc = qn_s[rows, :]
            chunks.append((first_chunk + k, rows, kc, qc, kc.astype(BF16), qc.astype(BF16)))
        kks = [_dot_nt(kcb, kcb) for (_, _, _, _, kcb, _) in chunks]
        qks = [_dot_nt(qcb, kcb) for (_, _, _, _, kcb, qcb) in chunks]
        probs = []
        for (ch, rows, kc, qc, _, _), kk, qk in zip(chunks, kks, qks):
            lgc = lg_ref[0, rows, :]
            gac = ga_s[rows, :]
            for hh in range(2):
                head = 2 * j + hh
                beta = jax.nn.sigmoid(select_lane(lgc, head))
                gc = jnp.broadcast_to(select_lane(gac, n_vh + head), (c, hd))
                s = 1
                while s < c:
                    gc = gc + jnp.where(pos >= s, pltpu.roll(gc, s, axis=0), 0.0)
                    s *= 2
                gl = jnp.broadcast_to(gc[c - 1:c, :], (c, hd))
                eg = jnp.exp(gc)
                row_b = jnp.broadcast_to(gcr_s[hh, pl.ds(ch, 1), :], (c, c))
                dec = jnp.exp(jnp.where(incl, gc[:, :c] - row_b, -jnp.inf))
                low = jnp.where(strict, kk * beta * dec, 0.0)
                rhs = jnp.concatenate([vc_s[rows, hh * hd:(hh + 1) * hd] * beta, kc * beta * eg], axis=1)
                probs.append(dict(low=low, rhs=rhs, at=(qk * dec).astype(BF16), qd=qc * eg,
                                  kd=(kc * jnp.exp(gl - gc)).astype(BF16), dg=jnp.exp(gl[:8, :])))
        yield
        eye = (ri == ci).astype(F32)
        ps = [_dot3(p["low"], p["low"]) for p in probs]
        xs = [eye - p["low"] for p in probs]
        xs = [x + _dot3(x, p) for x, p in zip(xs, ps)]
        yield
        n_terms = 4
        while n_terms < c:
            pbs = [p.astype(BF16) for p in ps]
            ps = [_dot(pb, pb) for pb in pbs]
            xs = [x + _dot(x.astype(BF16), p.astype(BF16)) for x, p in zip(xs, ps)]
            n_terms *= 2
            if n_terms in (16, 64):
                yield
        sols = [_dot3(x, p["rhs"]) for x, p in zip(xs, probs)]
        ubs = [sol[:, :hd].astype(BF16) for sol in sols]
        wbs = [sol[:, hd:].astype(BF16) for sol in sols]
        kws = [_dot_tn(p["kd"], wb) for p, wb in zip(probs, wbs)]
        nns = [_dot_tn(p["kd"], ub) for p, ub in zip(probs, ubs)]
        aws = [_dot(p["at"], wb) for p, wb in zip(probs, wbs)]
        ops = [_dot(p["at"], ub) for p, ub in zip(probs, ubs)]
        for n, p in enumerate(probs):
            k, hh = divmod(n, 2)
            kw_r[slot, k, hh] = kws[n].astype(BF16)
            nn_r[slot, k, hh] = nns[n]
            qp_r[slot, k, hh] = (p["qd"] - aws[n]).astype(BF16)
            op_r[slot, k, hh] = ops[n]
            dg_r[slot, k, hh] = p["dg"]

    def scan(ch, slot, k, states):
        rows = pl.ds(pl.multiple_of(ch * c, c), c)
        new_states = []
        for hh in range(2):
            st = states[hh]
            sb = st.astype(BF16)
            o = _dot(qp_r[slot, k, hh], sb) + op_r[slot, k, hh]
            decay = jnp.broadcast_to(dg_r[slot, k, hh][0:1, :], (hd, hd))
            new_states.append(st * decay - _dot(kw_r[slot, k, hh], sb) + nn_r[slot, k, hh])
            o = o * lax.rsqrt(jnp.mean(o * o, axis=-1, keepdims=True) + RMS_EPS) * nw_ref[...]
            cols = slice(hh * hd, (hh + 1) * hd)
            o_ref[0, rows, cols] = (o * _silu(z_ref[0, rows, cols])).astype(o_ref.dtype)
        return tuple(new_states)

    for _ in prepare_group(0, 0):
        pass

    def body(g, states):
        slot = g % 2
        stages = prepare_group((g + 1) * GDN_GROUP, 1 - slot)
        for k in range(GDN_GROUP):
            next(stages)
            states = scan(g * GDN_GROUP + k, slot, k, states)
        for _ in stages:
            pass
        return states

    zero = jnp.zeros((hd, hd), F32)
    states = lax.fori_loop(0, n_groups - 1, body, (zero, zero))
    last = n_groups - 1
    for k in range(GDN_GROUP):
        states = scan(last * GDN_GROUP + k, last % 2, k, states)


def _gdn(proj, lg, lgt, prm, prow, conv_wt, norm_w, n_qk_heads):
    bsz, seq, _ = proj.shape
    hq = n_qk_heads
    hd = HEAD_DIM
    n_chunks = seq // GDN_CHUNK
    ring = (2, GDN_GROUP, 2)
    return pl.pallas_call(
        _gdn_kernel,
        grid=(bsz, hq),
        in_specs=[pl.BlockSpec((1, seq, hd), lambda b, j: (b, 0, j)),
                  pl.BlockSpec((1, seq, hd), lambda b, j: (b, 0, hq + j)),
                  pl.BlockSpec((1, seq, 2 * hd), lambda b, j: (b, 0, hq + j)),
                  pl.BlockSpec((1, seq, 2 * hd), lambda b, j: (b, 0, 2 * hq + j)),
                  pl.BlockSpec((CONV_WIDTH, hd), lambda b, j: (0, j)),
                  pl.BlockSpec((CONV_WIDTH, hd), lambda b, j: (0, hq + j)),
                  pl.BlockSpec((CONV_WIDTH, 2 * hd), lambda b, j: (0, hq + j)),
                  pl.BlockSpec((1, seq, LANES), lambda b, j: (b, 0, 0)),
                  pl.BlockSpec((1, 2, n_chunks, GDN_CHUNK), lambda b, j: (b, j, 0, 0)),
                  pl.BlockSpec((8, LANES), lambda b, j: (0, 0)),
                  pl.BlockSpec((2, 2, GDN_CHUNK), lambda b, j: (j, 0, 0)),
                  pl.BlockSpec((1, hd), lambda b, j: (0, 0))],
        out_specs=pl.BlockSpec((1, seq, 2 * hd), lambda b, j: (b, 0, j)),
        out_shape=jax.ShapeDtypeStruct((bsz, seq, 2 * hq * hd), BF16),
        scratch_shapes=[pltpu.VMEM((seq, hd), F32), pltpu.VMEM((seq, hd), F32),
                        pltpu.VMEM((seq, 2 * hd), F32),
                        pltpu.VMEM((seq, LANES), F32),
                        pltpu.VMEM((2, n_chunks, GDN_CHUNK), F32),
                        pltpu.VMEM(ring + (hd, hd), BF16),
                        pltpu.VMEM(ring + (hd, hd), F32),
                        pltpu.VMEM(ring + (GDN_CHUNK, hd), BF16),
                        pltpu.VMEM(ring + (GDN_CHUNK, hd), F32),
                        pltpu.VMEM(ring + (8, hd), F32)],
        compiler_params=_cparams("arbitrary", "arbitrary"),
    )(proj, proj, proj, proj, conv_wt, conv_wt, conv_wt, lg, lgt, prm, prow, norm_w)


SB_HEADS_PER_STEP = 4
SB_ZERO_EXPONENT = -110.0


def _sb_kernel(q_ref, k_ref, v_ref, o_ref, kmax_s, *, blk):
    i = pl.program_id(2)
    hd = HEAD_DIM
    scale = hd ** -0.5
    heads = range(SB_HEADS_PER_STEP)

    @pl.when(i == 0)
    def _():
        for hh in heads:
            kf = k_ref[0, :, hh * hd:(hh + 1) * hd].astype(F32)
            n2 = jnp.max(jnp.sum(kf * kf, axis=1, keepdims=True), axis=0, keepdims=True)
            kmax_s[hh] = jnp.broadcast_to(jnp.sqrt(n2) * scale, kmax_s.shape[1:])

    ri = lax.broadcasted_iota(jnp.int32, (blk, blk), 0)
    ci = lax.broadcasted_iota(jnp.int32, (blk, blk), 1)
    tri = (ri >= ci).astype(BF16)
    causal = ci < ri
    qs = [q_ref[0, :, hh * hd:(hh + 1) * hd] for hh in heads]
    z_bound = []
    for hh in heads:
        qf = qs[hh].astype(F32)
        z_bound.append(jnp.sqrt(jnp.sum(qf * qf, axis=1, keepdims=True)) * kmax_s[hh][0:1, 0:1])

    def block(jb, accs, runs, diagonal):
        rows = pl.ds(pl.multiple_of(jb * blk, blk), blk)
        cols = [slice(hh * hd, (hh + 1) * hd) for hh in heads]
        zs = [_dot_nt(qs[hh], k_ref[0, rows, cols[hh]]) * scale for hh in heads]
        lss = [-_softplus(z) for z in zs]
        if diagonal:
            lss = [jnp.where(causal, ls, 0.0) for ls in lss]
        parts = [_split(ls) for ls in lss]
        rcs = [_dot(hi, tri) + _dot(lo, tri) for hi, lo in parts]
        ws = [jnp.exp(zs[hh] + rcs[hh] + runs[hh]) for hh in heads]
        if diagonal:
            ws = [jnp.where(causal, w, 0.0) for w in ws]
        new_accs = [accs[hh] + _dot(ws[hh].astype(BF16), v_ref[0, rows, cols[hh]]) for hh in heads]
        new_runs = [runs[hh] + rcs[hh][:, :1] for hh in heads]
        return tuple(new_accs), tuple(new_runs)

    def any_weight_left(runs):
        m = jnp.max(runs[0] + z_bound[0])
        for hh in heads[1:]:
            m = jnp.maximum(m, jnp.max(runs[hh] + z_bound[hh]))
        return (m > SB_ZERO_EXPONENT).astype(jnp.int32)

    zeros = tuple(jnp.zeros((blk, hd), F32) for _ in heads)
    zrun = tuple(jnp.zeros((blk, 1), F32) for _ in heads)
    accs, runs = block(i, zeros, zrun, True)

    def cond(carry):
        return jnp.logical_and(carry[0] >= 0, carry[1] > 0)

    def body(carry):
        jb, _, accs, runs = carry
        accs, runs = block(jb, accs, runs, False)
        return jb - 1, any_weight_left(runs), accs, runs

    _, _, accs, _ = lax.while_loop(cond, body, (i - 1, any_weight_left(runs), accs, runs))
    for hh in heads:
        o_ref[0, :, hh * hd:(hh + 1) * hd] = accs[hh].astype(o_ref.dtype)


def _sb_attention(q, kv, n_heads, blk=256):
    bsz, seq, _ = q.shape
    w = SB_HEADS_PER_STEP * HEAD_DIM
    n_hp = n_heads // SB_HEADS_PER_STEP
    return pl.pallas_call(
        functools.partial(_sb_kernel, blk=blk),
        grid=(bsz, n_hp, seq // blk),
        in_specs=[pl.BlockSpec((1, blk, w), lambda b, h, i: (b, i, h)),
                  pl.BlockSpec((1, seq, w), lambda b, h, i: (b, 0, h)),
                  pl.BlockSpec((1, seq, w), lambda b, h, i: (b, 0, n_hp + h))],
        out_specs=pl.BlockSpec((1, blk, w), lambda b, h, i: (b, i, h)),
        out_shape=jax.ShapeDtypeStruct((bsz, seq, n_heads * HEAD_DIM), BF16),
        scratch_shapes=[pltpu.VMEM((SB_HEADS_PER_STEP, 8, LANES), F32)],
        compiler_params=_cparams("arbitrary", "arbitrary", "arbitrary"),
    )(q, kv, kv)


def _route_kernel(lg_ref, br_ref, idx_ref, wt_ref, cnt_ref, carry_ref, *, tm):
    i = pl.program_id(0)

    @pl.when(i == 0)
    def _():
        carry_ref[...] = jnp.zeros_like(carry_ref)

    lane = lax.broadcasted_iota(jnp.int32, (tm, LANES), 1)
    lane_f = lane.astype(F32)
    logits = jnp.where(lane < N_EXPERTS, lg_ref[...] + br_ref[...], -jnp.inf)
    m1 = jnp.max(logits, axis=1, keepdims=True)
    i1 = jnp.min(jnp.where(logits == m1, lane_f, float(LANES)), axis=1, keepdims=True)
    oh1 = lane_f == i1
    rest = jnp.where(oh1, -jnp.inf, logits)
    m2 = jnp.max(rest, axis=1, keepdims=True)
    i2 = jnp.min(jnp.where(rest == m2, lane_f, float(LANES)), axis=1, keepdims=True)
    oh2 = lane_f == i2
    e21 = jnp.exp(m2 - m1)
    w1 = 1.0 / (1.0 + e21)
    w2 = e21 / (1.0 + e21)

    cnt = jnp.where(oh1, 1.0, jnp.where(oh2, 1.0, 0.0))
    before = (lax.broadcasted_iota(jnp.int32, (tm, tm), 1) < lax.broadcasted_iota(jnp.int32, (tm, tm), 0)).astype(BF16)
    rank = _dot(before, cnt.astype(BF16)) + carry_ref[0:1, :]
    r1 = jnp.sum(jnp.where(oh1, rank, 0.0), axis=1, keepdims=True)
    r2 = jnp.sum(jnp.where(oh2, rank, 0.0), axis=1, keepdims=True)
    total = carry_ref[0:1, :] + jnp.sum(cnt, axis=0, keepdims=True)
    carry_ref[...] = jnp.broadcast_to(total, carry_ref.shape)
    cnt_ref[...] = jnp.broadcast_to(total, cnt_ref.shape)

    packed = jnp.where(lane == 0, i1, jnp.where(lane == 1, i2, jnp.where(lane == 2, r1, jnp.where(lane == 3, r2, 0.0))))
    idx_ref[...] = packed.astype(jnp.int32)
    wt_ref[...] = jnp.where(lane == 0, w1, jnp.where(lane == 1, w2, 0.0))


def _route(logits, b_router_pad, tm=512):
    n = logits.shape[0]
    row = lambda i: (i, 0)
    return pl.pallas_call(
        functools.partial(_route_kernel, tm=tm),
        grid=(n // tm,),
        in_specs=[pl.BlockSpec((tm, LANES), row), pl.BlockSpec((1, LANES), lambda i: (0, 0))],
        out_specs=[pl.BlockSpec((tm, LANES), row), pl.BlockSpec((tm, LANES), row),
                   pl.BlockSpec((8, LANES), lambda i: (0, 0))],
        out_shape=[jax.ShapeDtypeStruct((n, LANES), jnp.int32), jax.ShapeDtypeStruct((n, LANES), F32),
                   jax.ShapeDtypeStruct((8, LANES), F32)],
        scratch_shapes=[pltpu.VMEM((8, LANES), F32)],
        compiler_params=_cparams("arbitrary"),
    )(logits, b_router_pad)


def _row_copy(src_hbm, src_row, dst, dst_row, sem):
    return pltpu.make_async_copy(src_hbm.at[pl.ds(src_row, 1)], dst.at[pl.ds(dst_row, 1)], sem)


def _gather_kernel(src_ref, h_hbm, o_ref, buf, sem, *, tg):
    base = pl.program_id(0) * tg

    def start(r, carry):
        _row_copy(h_hbm, src_ref[base + r], buf, r, sem).start()
        return carry

    lax.fori_loop(0, tg, start, 0, unroll=8)
    pltpu.make_async_copy(h_hbm.at[pl.ds(0, tg)], buf, sem).wait()
    o_ref[...] = buf[...].astype(o_ref.dtype)


def _gather_rows(h, src, tg=256):
    m_pad = src.shape[0]
    d = h.shape[1]
    return pl.pallas_call(
        functools.partial(_gather_kernel, tg=tg),
        grid_spec=pltpu.PrefetchScalarGridSpec(
            num_scalar_prefetch=1,
            grid=(m_pad // tg,),
            in_specs=[pl.BlockSpec(memory_space=pl.ANY)],
            out_specs=pl.BlockSpec((tg, d), lambda i, src: (i, 0)),
            scratch_shapes=[pltpu.VMEM((tg, d), h.dtype), pltpu.SemaphoreType.DMA(())]),
        out_shape=jax.ShapeDtypeStruct((m_pad, d), BF16),
        compiler_params=_cparams("arbitrary"),
    )(src, h)


def _group_swiglu_kernel(te_ref, tv_ref, a_ref, wg_ref, wu_ref, o_ref, wgb_ref, wub_ref):
    i = pl.program_id(1)
    new_expert = jnp.logical_or(i == 0, te_ref[i] != te_ref[jnp.maximum(i - 1, 0)])

    @pl.when(new_expert)
    def _():
        wgb_ref[...] = wg_ref[...].astype(BF16)
        wub_ref[...] = wu_ref[...].astype(BF16)

    @pl.when(tv_ref[i] > 0)
    def _():
        a = a_ref[...]
        g = _dot(a, wgb_ref[...])
        u = _dot(a, wub_ref[...])
        o_ref[...] = (_silu(g) * u).astype(o_ref.dtype)

    @pl.when(tv_ref[i] == 0)
    def _():
        o_ref[...] = jnp.zeros_like(o_ref)


def _group_swiglu(xs, w_gu, layer, tile_expert, tile_valid, tm, tn=512):
    m_pad, k = xs.shape
    f = w_gu.shape[-1] // 2
    nt = f // tn
    return pl.pallas_call(
        _group_swiglu_kernel,
        grid_spec=pltpu.PrefetchScalarGridSpec(
            num_scalar_prefetch=2,
            grid=(nt, m_pad // tm),
            in_specs=[pl.BlockSpec((tm, k), lambda n, i, te, tv: (i, 0)),
                      pl.BlockSpec((None, None, k, tn), lambda n, i, te, tv: (layer, te[i], 0, n)),
                      pl.BlockSpec((None, None, k, tn), lambda n, i, te, tv: (layer, te[i], 0, nt + n))],
            out_specs=pl.BlockSpec((tm, tn), lambda n, i, te, tv: (i, n)),
            scratch_shapes=[pltpu.VMEM((k, tn), BF16), pltpu.VMEM((k, tn), BF16)]),
        out_shape=jax.ShapeDtypeStruct((m_pad, f), BF16),
        compiler_params=_cparams("arbitrary", "arbitrary"),
    )(tile_expert, tile_valid, xs, w_gu, w_gu)


def _group_down_kernel(te_ref, tv_ref, a_ref, w_ref, o_ref, *, nk):
    i = pl.program_id(0)
    kk = pl.program_id(1)

    @pl.when(kk == 0)
    def _():
        o_ref[...] = jnp.zeros_like(o_ref)

    @pl.when(tv_ref[i] > 0)
    def _():
        o_ref[...] += _dot(a_ref[...], w_ref[...])


def _group_down(hm, w_down, tile_expert, tile_valid, tm, tk=1024):
    m_pad, f = hm.shape
    d = w_down.shape[-1]
    nk = f // tk
    return pl.pallas_call(
        functools.partial(_group_down_kernel, nk=nk),
        grid_spec=pltpu.PrefetchScalarGridSpec(
            num_scalar_prefetch=2,
            grid=(m_pad // tm, nk),
            in_specs=[pl.BlockSpec((tm, tk), lambda i, kk, te, tv: (i, kk)),
                      pl.BlockSpec((None, tk, d), lambda i, kk, te, tv: (te[i], kk, 0))],
            out_specs=pl.BlockSpec((tm, d), lambda i, kk, te, tv: (i, 0))),
        out_shape=jax.ShapeDtypeStruct((m_pad, d), F32),
        compiler_params=_cparams("arbitrary", "arbitrary"),
    )(tile_expert, tile_valid, hm, w_down)


def _combine_ln_kernel(d1_ref, d2_ref, o_hbm, wt_ref, x_ref, gate_ref, g_ref, b_ref, xo_ref, buf1, buf2, sem, *, tm):
    base = pl.program_id(0) * tm

    def start(r, carry):
        _row_copy(o_hbm, d1_ref[base + r], buf1, r, sem).start()
        _row_copy(o_hbm, d2_ref[base + r], buf2, r, sem).start()
        return carry

    lax.fori_loop(0, tm, start, 0, unroll=8)
    pltpu.make_async_copy(o_hbm.at[pl.ds(0, tm)], buf1, sem).wait()
    pltpu.make_async_copy(o_hbm.at[pl.ds(0, tm)], buf2, sem).wait()
    wt = wt_ref[...]
    y = wt[:, 0:1] * buf1[...] + wt[:, 1:2] * buf2[...]
    r = DN_ALPHA * x_ref[...] + gate_ref[0] * y
    xo_ref[...] = _layer_norm(r, g_ref[...], b_ref[...])


def _combine_ln(o_sorted, d1, d2, wts, xf, gate, ln_g, ln_b, seq, tm=256):
    n, d = xf.shape
    row = lambda i, d1, d2: (i, 0)
    const = lambda i, d1, d2: (0, 0)
    return pl.pallas_call(
        functools.partial(_combine_ln_kernel, tm=tm),
        grid_spec=pltpu.PrefetchScalarGridSpec(
            num_scalar_prefetch=2,
            grid=(n // tm,),
            in_specs=[pl.BlockSpec(memory_space=pl.ANY),
                      pl.BlockSpec((tm, LANES), row),
                      pl.BlockSpec((tm, d), row),
                      pl.BlockSpec((1, 1, d), lambda i, d1, d2: (i * tm // seq, 0, 0)),
                      pl.BlockSpec((1, d), const),
                      pl.BlockSpec((1, d), const)],
            out_specs=pl.BlockSpec((tm, d), row),
            scratch_shapes=[pltpu.VMEM((tm, d), F32), pltpu.VMEM((tm, d), F32), pltpu.SemaphoreType.DMA(())]),
        out_shape=jax.ShapeDtypeStruct((n, d), F32),
        compiler_params=_cparams("arbitrary"),
    )(d1, d2, o_sorted, wts, xf, gate, ln_g.reshape(1, d), ln_b.reshape(1, d))


def kernel(x, c, w_ada, b_ada, ln_g, ln_b, gdn_w_in, gdn_conv_w, gdn_a_log, gdn_dt_bias, gdn_norm_w, gdn_w_out,
           kv_w_ada, kv_b_ada, sb_w_kv, sb_w_q, sb_w_out, ffn_w_gu, ffn_w_down, moe_w_router, moe_b_router,
           moe_w_gu, moe_w_down):
    bsz, seq, d = x.shape
    n = bsz * seq
    assert w_ada.shape[0] == DEPTH and gdn_w_in.shape[0] == 1 and sb_w_q.shape[0] == 1
    n_vh = gdn_a_log.shape[1]
    n_qk = n_vh // 2
    key_dim = n_qk * HEAD_DIM
    value_dim = n_vh * HEAD_DIM
    main_cols = 2 * key_dim + 2 * value_dim
    n_sb_heads = sb_w_q.shape[-1] // HEAD_DIM
    xf = x.reshape(n, d)

    c_pad = jnp.zeros((8, d), F32).at[:bsz].set(c)
    def mod_vectors(w, layer, b, parts):
        m = _ada(c_pad, w, layer, b)[:bsz]
        return [v.reshape(bsz, 1, d) for v in jnp.split(m, parts, axis=-1)]
    sh_m0, sc_m0, gt_m0, sh_f0, sc_f0, gt_f0 = mod_vectors(w_ada, 0, b_ada[0], 6)
    sh_m1, sc_m1, gt_m1, sh_f1, sc_f1, gt_f1 = mod_vectors(w_ada, 1, b_ada[1], 6)
    sh_kv, sc_kv = mod_vectors(kv_w_ada[None], 0, kv_b_ada, 2)

    h = _modulate(xf, sh_m0, sc_m0, seq)
    proj = _ws_mm(h, gdn_w_in, 0, main_cols, F32)
    w_gate = jnp.zeros((1, d, LANES), F32).at[0, :, :2 * n_vh].set(gdn_w_in[0, :, main_cols:])
    gate_logits = _ws_mm(h, w_gate, 0, LANES, F32)
    prm = jnp.zeros((8, LANES), F32)
    prm = prm.at[0, n_vh:2 * n_vh].set(gdn_dt_bias[0].astype(F32))
    prm = prm.at[1, n_vh:2 * n_vh].set(-jnp.exp(gdn_a_log[0].astype(F32)))
    conv_wt = jnp.transpose(gdn_conv_w[0])
    lg3 = gate_logits.reshape(bsz, seq, LANES)
    lgt = jnp.transpose(lg3[:, :, n_vh:2 * n_vh], (0, 2, 1)).reshape(bsz, n_vh, seq // GDN_CHUNK, GDN_CHUNK)
    prow = jnp.broadcast_to(prm[0:2, n_vh:2 * n_vh].T[:, :, None], (n_vh, 2, GDN_CHUNK))
    o = _gdn(proj.reshape(bsz, seq, main_cols), lg3, lgt, prm, prow, conv_wt,
             gdn_norm_w[0].reshape(1, HEAD_DIM), n_qk)
    x1, h1 = _proj_ln(o.reshape(n, value_dim), gdn_w_out[0].astype(BF16), xf, gt_m0, ln_g[0, 0], ln_b[0, 0],
                      [(sh_f0, sc_f0)], seq)

    hm = _ws_swiglu(h1, ffn_w_gu, 0)
    x2, h2, h_kv = _proj_ln(hm, ffn_w_down[0].astype(BF16), x1, gt_f0, ln_g[0, 1], ln_b[0, 1],
                            [(sh_m1, sc_m1), (sh_kv, sc_kv)], seq)

    kv = _ws_mm(h_kv, sb_w_kv[None], 0, 2 * n_sb_heads * HEAD_DIM, BF16)
    q = _ws_mm(h2, sb_w_q, 0, n_sb_heads * HEAD_DIM, BF16)
    oa = _sb_attention(q.reshape(bsz, seq, -1), kv.reshape(bsz, seq, -1), n_sb_heads)
    w_router = jnp.zeros((d, LANES), F32).at[:, :N_EXPERTS].set(moe_w_router[0])
    x3, h3, logits = _proj_ln(oa.reshape(n, -1), sb_w_out[0].astype(BF16), x2, gt_m1, ln_g[1, 0], ln_b[1, 0],
                              [(sh_f1, sc_f1)], seq, h_dtype=F32, w_router=w_router)

    b_router = jnp.zeros((1, LANES), F32).at[0, :N_EXPERTS].set(moe_b_router[0].astype(F32))
    idx, wts, cnt = _route(logits, b_router)
    tm_g = 512
    n_tiles = 2 * n // tm_g + N_EXPERTS
    counts = cnt[0, :N_EXPERTS].astype(jnp.int32)
    tiles_per = (counts + tm_g - 1) // tm_g
    tile_end = jnp.cumsum(tiles_per)
    row_off = (tile_end - tiles_per) * tm_g
    tile_ids = jnp.arange(n_tiles, dtype=jnp.int32)
    tile_expert = jnp.minimum(jnp.searchsorted(tile_end, tile_ids, side="right"), N_EXPERTS - 1).astype(jnp.int32)
    tile_valid = (tile_ids < tile_end[-1]).astype(jnp.int32)
    dest1 = row_off[idx[:, 0]] + idx[:, 2]
    dest2 = row_off[idx[:, 1]] + idx[:, 3]
    tok = jnp.arange(n, dtype=jnp.int32)
    src = jnp.zeros((n_tiles * tm_g,), jnp.int32).at[dest1].set(tok).at[dest2].set(tok)

    xs = _gather_rows(h3, src)
    hm2 = _group_swiglu(xs, moe_w_gu, 0, tile_expert, tile_valid, tm_g)
    o_sorted = _group_down(hm2, moe_w_down[0].astype(BF16), tile_expert, tile_valid, tm_g)
    out = _combine_ln(o_sorted, dest1, dest2, wts, x3, gt_f1, ln_g[1, 1], ln_b[1, 1], seq)
    return out.reshape(bsz, seq, d)
```

```python
import functools

import jax
import jax.numpy as jnp
from jax import lax
from jax.experimental import pallas as pl
from jax.experimental.pallas import tpu as pltpu

F32 = jnp.float32
BF16 = jnp.bfloat16

LANES = 128
VMEM_LIMIT_BYTES = 56 * 1024 * 1024

HEAD_DIM = 128
GDN_CHUNK = 64
CONV_WIDTH = 4
N_EXPERTS = 8
MOE_TILE = 1024
MOE_ROW_STEP = 256
LN_EPS = 1e-5
RMS_EPS = 1e-6
L2_EPS = 1e-6
DEPTH = 2
DN_ALPHA = (2.0 * DEPTH) ** 0.25


def _cparams(*sem):
    return pltpu.CompilerParams(dimension_semantics=sem, vmem_limit_bytes=VMEM_LIMIT_BYTES)


def _dot(a, b):
    return jnp.dot(a, b, preferred_element_type=F32)


def _dot_nt(a, b):
    return lax.dot_general(a, b, (((1,), (1,)), ((), ())), preferred_element_type=F32)


def _dot_tn(a, b):
    return lax.dot_general(a, b, (((0,), (0,)), ((), ())), preferred_element_type=F32)


def _split(a):
    hi = a.astype(BF16)
    lo = (a - hi.astype(F32)).astype(BF16)
    return hi, lo


def _dot3(a, b):
    ah, al = _split(a)
    bh, bl = _split(b)
    return _dot(ah, bh) + (_dot(ah, bl) + _dot(al, bh))


def _softplus(x):
    return jnp.maximum(x, 0.0) + jnp.log(1.0 + jnp.exp(-jnp.abs(x)))


def _silu(x):
    return x * jax.nn.sigmoid(x)


def _ada_kernel(c_ref, w_ref, b_ref, o_ref):
    s = _silu(c_ref[...]).astype(BF16)
    o_ref[...] = _dot(s, w_ref[...].astype(BF16)) + b_ref[...]


def _ada(c_pad, w, layer, b, tn=1024):
    _, d, n_out = w.shape
    return pl.pallas_call(
        _ada_kernel,
        grid=(n_out // tn,),
        in_specs=[pl.BlockSpec((8, d), lambda n: (0, 0)),
                  pl.BlockSpec((None, d, tn), lambda n: (layer, 0, n)),
                  pl.BlockSpec((1, tn), lambda n: (0, n))],
        out_specs=pl.BlockSpec((8, tn), lambda n: (0, n)),
        out_shape=jax.ShapeDtypeStruct((8, n_out), F32),
        compiler_params=_cparams("arbitrary"),
    )(c_pad, w, b.reshape(1, n_out))


def _modulate_kernel(x_ref, sh_ref, sc_ref, o_ref):
    o_ref[...] = (x_ref[...] * (1.0 + sc_ref[0]) + sh_ref[0]).astype(o_ref.dtype)


def _modulate(xf, shift, scale, seq, tm=512):
    n, d = xf.shape
    bmap = lambda i: (i * tm // seq, 0, 0)
    return pl.pallas_call(
        _modulate_kernel,
        grid=(n // tm,),
        in_specs=[pl.BlockSpec((tm, d), lambda i: (i, 0)),
                  pl.BlockSpec((1, 1, d), bmap),
                  pl.BlockSpec((1, 1, d), bmap)],
        out_specs=pl.BlockSpec((tm, d), lambda i: (i, 0)),
        out_shape=jax.ShapeDtypeStruct((n, d), BF16),
        compiler_params=_cparams("arbitrary"),
    )(xf, shift, scale)


def _ws_mm_kernel(a_ref, w_ref, o_ref, wb_ref):
    @pl.when(pl.program_id(1) == 0)
    def _():
        wb_ref[...] = w_ref[...].astype(BF16)

    o_ref[...] = _dot(a_ref[...], wb_ref[...]).astype(o_ref.dtype)


def _ws_mm(a, w, layer, n_out, out_dtype, tm=1024, tn=1024):
    m, k = a.shape
    tn = min(tn, n_out)
    return pl.pallas_call(
        _ws_mm_kernel,
        grid=(n_out // tn, m // tm),
        in_specs=[pl.BlockSpec((tm, k), lambda n, i: (i, 0)),
                  pl.BlockSpec((None, k, tn), lambda n, i: (layer, 0, n))],
        out_specs=pl.BlockSpec((tm, tn), lambda n, i: (i, n)),
        out_shape=jax.ShapeDtypeStruct((m, n_out), out_dtype),
        scratch_shapes=[pltpu.VMEM((k, tn), BF16)],
        compiler_params=_cparams("arbitrary", "arbitrary"),
    )(a, w)


def _ws_swiglu_kernel(a_ref, wg_ref, wu_ref, o_ref, wgb_ref, wub_ref):
    @pl.when(pl.program_id(1) == 0)
    def _():
        wgb_ref[...] = wg_ref[...].astype(BF16)
        wub_ref[...] = wu_ref[...].astype(BF16)

    a = a_ref[...]
    g = _dot(a, wgb_ref[...])
    u = _dot(a, wub_ref[...])
    o_ref[...] = (_silu(g) * u).astype(o_ref.dtype)


def _ws_swiglu(a, w_gu, layer, tm=1024, tn=512):
    m, k = a.shape
    f = w_gu.shape[-1] // 2
    nt = f // tn
    return pl.pallas_call(
        _ws_swiglu_kernel,
        grid=(nt, m // tm),
        in_specs=[pl.BlockSpec((tm, k), lambda n, i: (i, 0)),
                  pl.BlockSpec((None, k, tn), lambda n, i: (layer, 0, n)),
                  pl.BlockSpec((None, k, tn), lambda n, i: (layer, 0, nt + n))],
        out_specs=pl.BlockSpec((tm, tn), lambda n, i: (i, n)),
        out_shape=jax.ShapeDtypeStruct((m, f), BF16),
        scratch_shapes=[pltpu.VMEM((k, tn), BF16), pltpu.VMEM((k, tn), BF16)],
        compiler_params=_cparams("arbitrary", "arbitrary"),
    )(a, w_gu, w_gu)


def _layer_norm(r, g, b):
    mu = jnp.mean(r, axis=-1, keepdims=True)
    rc = r - mu
    var = jnp.mean(rc * rc, axis=-1, keepdims=True)
    return rc * lax.rsqrt(var + LN_EPS) * g + b


def _proj_ln_kernel(*refs, n_mod, nk, with_router):
    a_ref, w_ref, x_ref, gate_ref, g_ref, b_ref = refs[:6]
    mod_refs = refs[6:6 + 2 * n_mod]
    pos = 6 + 2 * n_mod
    wr_ref = refs[pos] if with_router else None
    pos += int(with_router)
    xo_ref = refs[pos]
    h_refs = refs[pos + 1:pos + 1 + n_mod]
    pos += 1 + n_mod
    lg_ref = refs[pos] if with_router else None
    acc_ref = refs[-1]
    kk = pl.program_id(1)

    @pl.when(kk == 0)
    def _():
        acc_ref[...] = jnp.zeros_like(acc_ref)

    acc_ref[...] += _dot(a_ref[...], w_ref[...])

    @pl.when(kk == nk - 1)
    def _():
        r = DN_ALPHA * x_ref[...] + gate_ref[0] * acc_ref[...]
        xn = _layer_norm(r, g_ref[...], b_ref[...])
        xo_ref[...] = xn
        for t in range(n_mod):
            h = xn * (1.0 + mod_refs[2 * t + 1][0]) + mod_refs[2 * t][0]
            h_refs[t][...] = h.astype(h_refs[t].dtype)
            if with_router and t == 0:
                lg_ref[...] = _dot3(h, wr_ref[...])


def _proj_ln(a, w, xf, gate, ln_g, ln_b, mods, seq, h_dtype=BF16, w_router=None, tm=512, tk=1024):
    m, k = a.shape
    d = w.shape[-1]
    nk = k // tk
    n_mod = len(mods)
    with_router = w_router is not None
    bmap = lambda i, kk: (i * tm // seq, 0, 0)
    row = lambda i, kk: (i, 0)
    const = lambda i, kk: (0, 0)
    in_specs = [pl.BlockSpec((tm, tk), lambda i, kk: (i, kk)),
                pl.BlockSpec((tk, d), lambda i, kk: (kk, 0)),
                pl.BlockSpec((tm, d), row),
                pl.BlockSpec((1, 1, d), bmap),
                pl.BlockSpec((1, d), const),
                pl.BlockSpec((1, d), const)]
    args = [a, w, xf, gate, ln_g.reshape(1, d), ln_b.reshape(1, d)]
    for sh, sc in mods:
        in_specs += [pl.BlockSpec((1, 1, d), bmap), pl.BlockSpec((1, 1, d), bmap)]
        args += [sh, sc]
    out_specs = [pl.BlockSpec((tm, d), row)] + [pl.BlockSpec((tm, d), row)] * n_mod
    out_shape = [jax.ShapeDtypeStruct((m, d), F32)] + [jax.ShapeDtypeStruct((m, d), h_dtype)] * n_mod
    if with_router:
        in_specs.append(pl.BlockSpec((d, LANES), const))
        args.append(w_router)
        out_specs.append(pl.BlockSpec((tm, LANES), row))
        out_shape.append(jax.ShapeDtypeStruct((m, LANES), F32))
    return pl.pallas_call(
        functools.partial(_proj_ln_kernel, n_mod=n_mod, nk=nk, with_router=with_router),
        grid=(m // tm, nk),
        in_specs=in_specs,
        out_specs=out_specs,
        out_shape=out_shape,
        scratch_shapes=[pltpu.VMEM((tm, d), F32)],
        compiler_params=_cparams("arbitrary", "arbitrary"),
    )(*args)


GDN_GROUP = 4


def _gdn_kernel(q_ref, k_ref, v_ref, z_ref, cwq_ref, cwk_ref, cwv_ref, lg_ref, lgt_ref, prm_ref, prow_ref, nw_ref,
                o_ref, qn_s, kn_s, vc_s, ga_s, gcr_s, kw_r, nn_r, qp_r, op_r, dg_r):
    j = pl.program_id(1)
    seq = q_ref.shape[1]
    c = GDN_CHUNK
    hd = HEAD_DIM
    n_groups = seq // (c * GDN_GROUP)
    n_vh = 2 * pl.num_programs(1)
    row = lax.broadcasted_iota(jnp.int32, (seq, 1), 0)

    def conv_silu(x, cw):
        acc = x * cw[CONV_WIDTH - 1:CONV_WIDTH, :]
        for s in range(1, CONV_WIDTH):
            xs = jnp.where(row >= s, pltpu.roll(x, s, axis=0), 0.0)
            acc = acc + xs * cw[CONV_WIDTH - 1 - s:CONV_WIDTH - s, :]
        return _silu(acc)

    def l2norm(x):
        return x * lax.rsqrt(jnp.sum(x * x, axis=-1, keepdims=True) + L2_EPS)

    qn_s[...] = l2norm(conv_silu(q_ref[0], cwq_ref[...])) * (hd ** -0.5)
    kn_s[...] = l2norm(conv_silu(k_ref[0], cwk_ref[...]))
    vc_s[...] = conv_silu(v_ref[0], cwv_ref[...])
    ga_s[...] = prm_ref[1:2, :] * _softplus(lg_ref[0] + prm_ref[0:1, :])

    ri = lax.broadcasted_iota(jnp.int32, (c, c), 0)
    ci = lax.broadcasted_iota(jnp.int32, (c, c), 1)
    upper = (ri <= ci).astype(BF16)
    for hh in range(2):
        g_row = prow_ref[hh, 1:2, :] * _softplus(lgt_ref[0, hh] + prow_ref[hh, 0:1, :])
        g1, g2 = _split(g_row)
        g3 = (g_row - g1.astype(F32) - g2.astype(F32)).astype(BF16)
        gcr_s[hh] = _dot(g1, upper) + (_dot(g2, upper) + _dot(g3, upper))

    incl = ri >= ci
    strict = ri > ci
    lane = lax.broadcasted_iota(jnp.int32, (c, LANES), 1)
    pos = lax.broadcasted_iota(jnp.int32, (c, 1), 0)

    def select_lane(x, idx):
        return jnp.sum(jnp.where(lane == idx, x, 0.0), axis=1, keepdims=True)

    def prepare_group(first_chunk, slot):
        chunks = []
        for k in range(GDN_GROUP):
            rows = pl.ds(pl.multiple_of((first_chunk + k) * c, c), c)
            kc = kn_s[rows, :]
            qc = qn_s[rows, :]
            chunks.append((first_chunk + k, rows, kc, qc, kc.astype(BF16), qc.astype(BF16)))
        kks = [_dot_nt(kcb, kcb) for (_, _, _, _, kcb, _) in chunks]
        qks = [_dot_nt(qcb, kcb) for (_, _, _, _, kcb, qcb) in chunks]
        probs = []
        for (ch, rows, kc, qc, _, _), kk, qk in zip(chunks, kks, qks):
            lgc = lg_ref[0, rows, :]
            gac = ga_s[rows, :]
            for hh in range(2):
                head = 2 * j + hh
                beta = jax.nn.sigmoid(select_lane(lgc, head))
                gc = jnp.broadcast_to(select_lane(gac, n_vh + head), (c, hd))
                s = 1
                while s < c:
                    gc = gc + jnp.where(pos >= s, pltpu.roll(gc, s, axis=0), 0.0)
                    s *= 2
                gl = jnp.broadcast_to(gc[c - 1:c, :], (c, hd))
                eg = jnp.exp(gc)
                row_b = jnp.broadcast_to(gcr_s[hh, pl.ds(ch, 1), :], (c, c))
                dec = jnp.exp(jnp.where(incl, gc[:, :c] - row_b, -jnp.inf))
                low = jnp.where(strict, kk * beta * dec, 0.0)
                rhs = jnp.concatenate([vc_s[rows, hh * hd:(hh + 1) * hd] * beta, kc * beta * eg], axis=1)
                probs.append(dict(low=low, rhs=rhs, at=(qk * dec).astype(BF16), qd=qc * eg,
                                  kd=(kc * jnp.exp(gl - gc)).astype(BF16), dg=jnp.exp(gl[:8, :])))
        yield
        eye = (ri == ci).astype(F32)
        ps = [p["low"] for p in probs]
        xs = [eye - p["low"] for p in probs]
        n_terms = 2
        while n_terms < c:
            pbs = [p.astype(BF16) for p in ps]
            ps = [_dot(pb, pb) for pb in pbs]
            xs = [x + _dot(x.astype(BF16), p.astype(BF16)) for x, p in zip(xs, ps)]
            n_terms *= 2
            if n_terms in (4, 16, 64):
                yield
        sols = [_dot(x.astype(BF16), p["rhs"].astype(BF16)) for x, p in zip(xs, probs)]
        ubs = [sol[:, :hd].astype(BF16) for sol in sols]
        wbs = [sol[:, hd:].astype(BF16) for sol in sols]
        kws = [_dot_tn(p["kd"], wb) for p, wb in zip(probs, wbs)]
        nns = [_dot_tn(p["kd"], ub) for p, ub in zip(probs, ubs)]
        aws = [_dot(p["at"], wb) for p, wb in zip(probs, wbs)]
        ops = [_dot(p["at"], ub) for p, ub in zip(probs, ubs)]
        for n, p in enumerate(probs):
            k, hh = divmod(n, 2)
            kw_r[slot, k, hh] = kws[n].astype(BF16)
            nn_r[slot, k, hh] = nns[n]
            qp_r[slot, k, hh] = (p["qd"] - aws[n]).astype(BF16)
            op_r[slot, k, hh] = ops[n]
            dg_r[slot, k, hh] = p["dg"]

    def scan(ch, slot, k, states):
        rows = pl.ds(pl.multiple_of(ch * c, c), c)
        new_states = []
        for hh in range(2):
            st = states[hh]
            sb = st.astype(BF16)
            o = _dot(qp_r[slot, k, hh], sb) + op_r[slot, k, hh]
            decay = jnp.broadcast_to(dg_r[slot, k, hh][0:1, :], (hd, hd))
            new_states.append(st * decay - _dot(kw_r[slot, k, hh], sb) + nn_r[slot, k, hh])
            o = o * lax.rsqrt(jnp.mean(o * o, axis=-1, keepdims=True) + RMS_EPS) * nw_ref[...]
            cols = slice(hh * hd, (hh + 1) * hd)
            o_ref[0, rows, cols] = (o * _silu(z_ref[0, rows, cols])).astype(o_ref.dtype)
        return tuple(new_states)

    for _ in prepare_group(0, 0):
        pass

    def body(g, states):
        slot = g % 2
        stages = prepare_group((g + 1) * GDN_GROUP, 1 - slot)
        for k in range(GDN_GROUP):
            next(stages)
            states = scan(g * GDN_GROUP + k, slot, k, states)
        for _ in stages:
            pass
        return states

    zero = jnp.zeros((hd, hd), F32)
    states = lax.fori_loop(0, n_groups - 1, body, (zero, zero))
    last = n_groups - 1
    for k in range(GDN_GROUP):
        states = scan(last * GDN_GROUP + k, last % 2, k, states)


def _gdn(proj, lg, lgt, prm, prow, conv_wt, norm_w, n_qk_heads):
    bsz, seq, _ = proj.shape
    hq = n_qk_heads
    hd = HEAD_DIM
    n_chunks = seq // GDN_CHUNK
    ring = (2, GDN_GROUP, 2)
    return pl.pallas_call(
        _gdn_kernel,
        grid=(bsz, hq),
        in_specs=[pl.BlockSpec((1, seq, hd), lambda b, j: (b, 0, j)),
                  pl.BlockSpec((1, seq, hd), lambda b, j: (b, 0, hq + j)),
                  pl.BlockSpec((1, seq, 2 * hd), lambda b, j: (b, 0, hq + j)),
                  pl.BlockSpec((1, seq, 2 * hd), lambda b, j: (b, 0, 2 * hq + j)),
                  pl.BlockSpec((CONV_WIDTH, hd), lambda b, j: (0, j)),
                  pl.BlockSpec((CONV_WIDTH, hd), lambda b, j: (0, hq + j)),
                  pl.BlockSpec((CONV_WIDTH, 2 * hd), lambda b, j: (0, hq + j)),
                  pl.BlockSpec((1, seq, LANES), lambda b, j: (b, 0, 0)),
                  pl.BlockSpec((1, 2, n_chunks, GDN_CHUNK), lambda b, j: (b, j, 0, 0)),
                  pl.BlockSpec((8, LANES), lambda b, j: (0, 0)),
                  pl.BlockSpec((2, 2, GDN_CHUNK), lambda b, j: (j, 0, 0)),
                  pl.BlockSpec((1, hd), lambda b, j: (0, 0))],
        out_specs=pl.BlockSpec((1, seq, 2 * hd), lambda b, j: (b, 0, j)),
        out_shape=jax.ShapeDtypeStruct((bsz, seq, 2 * hq * hd), BF16),
        scratch_shapes=[pltpu.VMEM((seq, hd), F32), pltpu.VMEM((seq, hd), F32),
                        pltpu.VMEM((seq, 2 * hd), F32),
                        pltpu.VMEM((seq, LANES), F32),
                        pltpu.VMEM((2, n_chunks, GDN_CHUNK), F32),
                        pltpu.VMEM(ring + (hd, hd), BF16),
                        pltpu.VMEM(ring + (hd, hd), F32),
                        pltpu.VMEM(ring + (GDN_CHUNK, hd), BF16),
                        pltpu.VMEM(ring + (GDN_CHUNK, hd), F32),
                        pltpu.VMEM(ring + (8, hd), F32)],
        compiler_params=_cparams("arbitrary", "arbitrary"),
    )(proj, proj, proj, proj, conv_wt, conv_wt, conv_wt, lg, lgt, prm, prow, norm_w)


SB_HEADS_PER_STEP = 4
SB_ZERO_EXPONENT = -110.0


def _sb_kernel(q_ref, k_ref, v_ref, o_ref, kmax_s, *, blk):
    i = pl.program_id(2)
    hd = HEAD_DIM
    scale = hd ** -0.5
    heads = range(SB_HEADS_PER_STEP)

    @pl.when(i == 0)
    def _():
        for hh in heads:
            kf = k_ref[0, :, hh * hd:(hh + 1) * hd].astype(F32)
            n2 = jnp.max(jnp.sum(kf * kf, axis=1, keepdims=True), axis=0, keepdims=True)
            kmax_s[hh] = jnp.broadcast_to(jnp.sqrt(n2) * scale, kmax_s.shape[1:])

    ri = lax.broadcasted_iota(jnp.int32, (blk, blk), 0)
    ci = lax.broadcasted_iota(jnp.int32, (blk, blk), 1)
    tri = (ri >= ci).astype(BF16)
    causal = ci < ri
    qs = [q_ref[0, :, hh * hd:(hh + 1) * hd] for hh in heads]
    z_bound = []
    for hh in heads:
        qf = qs[hh].astype(F32)
        z_bound.append(jnp.sqrt(jnp.sum(qf * qf, axis=1, keepdims=True)) * kmax_s[hh][0:1, 0:1])

    def block(jb, accs, runs, diagonal):
        rows = pl.ds(pl.multiple_of(jb * blk, blk), blk)
        cols = [slice(hh * hd, (hh + 1) * hd) for hh in heads]
        zs = [_dot_nt(qs[hh], k_ref[0, rows, cols[hh]]) * scale for hh in heads]
        lss = [-_softplus(z) for z in zs]
        if diagonal:
            lss = [jnp.where(causal, ls, 0.0) for ls in lss]
        parts = [_split(ls) for ls in lss]
        rcs = [_dot(hi, tri) + _dot(lo, tri) for hi, lo in parts]
        ws = [jnp.exp(zs[hh] + rcs[hh] + runs[hh]) for hh in heads]
        if diagonal:
            ws = [jnp.where(causal, w, 0.0) for w in ws]
        new_accs = [accs[hh] + _dot(ws[hh].astype(BF16), v_ref[0, rows, cols[hh]]) for hh in heads]
        new_runs = [runs[hh] + rcs[hh][:, :1] for hh in heads]
        return tuple(new_accs), tuple(new_runs)

    def any_weight_left(runs):
        m = jnp.max(runs[0] + z_bound[0])
        for hh in heads[1:]:
            m = jnp.maximum(m, jnp.max(runs[hh] + z_bound[hh]))
        return (m > SB_ZERO_EXPONENT).astype(jnp.int32)

    zeros = tuple(jnp.zeros((blk, hd), F32) for _ in heads)
    zrun = tuple(jnp.zeros((blk, 1), F32) for _ in heads)
    accs, runs = block(i, zeros, zrun, True)

    def cond(carry):
        return jnp.logical_and(carry[0] >= 0, carry[1] > 0)

    def body(carry):
        jb, _, accs, runs = carry
        accs, runs = block(jb, accs, runs, False)
        return jb - 1, any_weight_left(runs), accs, runs

    _, _, accs, _ = lax.while_loop(cond, body, (i - 1, any_weight_left(runs), accs, runs))
    for hh in heads:
        o_ref[0, :, hh * hd:(hh + 1) * hd] = accs[hh].astype(o_ref.dtype)


def _sb_attention(q, kv, n_heads, blk=256):
    bsz, seq, _ = q.shape
    w = SB_HEADS_PER_STEP * HEAD_DIM
    n_hp = n_heads // SB_HEADS_PER_STEP
    return pl.pallas_call(
        functools.partial(_sb_kernel, blk=blk),
        grid=(bsz, n_hp, seq // blk),
        in_specs=[pl.BlockSpec((1, blk, w), lambda b, h, i: (b, i, h)),
                  pl.BlockSpec((1, seq, w), lambda b, h, i: (b, 0, h)),
                  pl.BlockSpec((1, seq, w), lambda b, h, i: (b, 0, n_hp + h))],
        out_specs=pl.BlockSpec((1, blk, w), lambda b, h, i: (b, i, h)),
        out_shape=jax.ShapeDtypeStruct((bsz, seq, n_heads * HEAD_DIM), BF16),
        scratch_shapes=[pltpu.VMEM((SB_HEADS_PER_STEP, 8, LANES), F32)],
        compiler_params=_cparams("arbitrary", "arbitrary", "arbitrary"),
    )(q, kv, kv)


def _route_kernel(lg_ref, br_ref, idx_ref, wt_ref, cnt_ref, carry_ref, *, tm):
    i = pl.program_id(0)

    @pl.when(i == 0)
    def _():
        carry_ref[...] = jnp.zeros_like(carry_ref)

    lane = lax.broadcasted_iota(jnp.int32, (tm, LANES), 1)
    lane_f = lane.astype(F32)
    logits = jnp.where(lane < N_EXPERTS, lg_ref[...] + br_ref[...], -jnp.inf)
    m1 = jnp.max(logits, axis=1, keepdims=True)
    i1 = jnp.min(jnp.where(logits == m1, lane_f, float(LANES)), axis=1, keepdims=True)
    oh1 = lane_f == i1
    rest = jnp.where(oh1, -jnp.inf, logits)
    m2 = jnp.max(rest, axis=1, keepdims=True)
    i2 = jnp.min(jnp.where(rest == m2, lane_f, float(LANES)), axis=1, keepdims=True)
    oh2 = lane_f == i2
    e21 = jnp.exp(m2 - m1)
    w1 = 1.0 / (1.0 + e21)
    w2 = e21 / (1.0 + e21)

    cnt = jnp.where(oh1, 1.0, jnp.where(oh2, 1.0, 0.0))
    before = (lax.broadcasted_iota(jnp.int32, (tm, tm), 1) < lax.broadcasted_iota(jnp.int32, (tm, tm), 0)).astype(BF16)
    rank = _dot(before, cnt.astype(BF16)) + carry_ref[0:1, :]
    r1 = jnp.sum(jnp.where(oh1, rank, 0.0), axis=1, keepdims=True)
    r2 = jnp.sum(jnp.where(oh2, rank, 0.0), axis=1, keepdims=True)
    total = carry_ref[0:1, :] + jnp.sum(cnt, axis=0, keepdims=True)
    carry_ref[...] = jnp.broadcast_to(total, carry_ref.shape)
    cnt_ref[...] = jnp.broadcast_to(total, cnt_ref.shape)

    packed = jnp.where(lane == 0, i1, jnp.where(lane == 1, i2, jnp.where(lane == 2, r1, jnp.where(lane == 3, r2, 0.0))))
    idx_ref[...] = packed.astype(jnp.int32)
    wt_ref[...] = jnp.where(lane == 0, w1, jnp.where(lane == 1, w2, 0.0))


def _route(logits, b_router_pad, tm=512):
    n = logits.shape[0]
    row = lambda i: (i, 0)
    return pl.pallas_call(
        functools.partial(_route_kernel, tm=tm),
        grid=(n // tm,),
        in_specs=[pl.BlockSpec((tm, LANES), row), pl.BlockSpec((1, LANES), lambda i: (0, 0))],
        out_specs=[pl.BlockSpec((tm, LANES), row), pl.BlockSpec((tm, LANES), row),
                   pl.BlockSpec((8, LANES), lambda i: (0, 0))],
        out_shape=[jax.ShapeDtypeStruct((n, LANES), jnp.int32), jax.ShapeDtypeStruct((n, LANES), F32),
                   jax.ShapeDtypeStruct((8, LANES), F32)],
        scratch_shapes=[pltpu.VMEM((8, LANES), F32)],
        compiler_params=_cparams("arbitrary"),
    )(logits, b_router_pad)


def _row_copy(src_hbm, src_row, dst, dst_row, sem):
    return pltpu.make_async_copy(src_hbm.at[pl.ds(src_row, 1)], dst.at[pl.ds(dst_row, 1)], sem)


def _gather_kernel(src_ref, used_ref, h_hbm, o_ref, buf, sem, *, tg):
    i = pl.program_id(0)
    base = i * tg

    @pl.when(used_ref[i] > 0)
    def _():
        def start(r, carry):
            _row_copy(h_hbm, src_ref[base + r], buf, r, sem).start()
            return carry

        lax.fori_loop(0, tg, start, 0, unroll=8)
        pltpu.make_async_copy(h_hbm.at[pl.ds(0, tg)], buf, sem).wait()
        o_ref[...] = buf[...].astype(o_ref.dtype)

    @pl.when(used_ref[i] == 0)
    def _():
        o_ref[...] = jnp.zeros_like(o_ref)


def _gather_rows(h, src, piece_used):
    tg = MOE_ROW_STEP
    m_pad = src.shape[0]
    d = h.shape[1]
    return pl.pallas_call(
        functools.partial(_gather_kernel, tg=tg),
        grid_spec=pltpu.PrefetchScalarGridSpec(
            num_scalar_prefetch=2,
            grid=(m_pad // tg,),
            in_specs=[pl.BlockSpec(memory_space=pl.ANY)],
            out_specs=pl.BlockSpec((tg, d), lambda i, src, used: (i, 0)),
            scratch_shapes=[pltpu.VMEM((tg, d), h.dtype), pltpu.SemaphoreType.DMA(())]),
        out_shape=jax.ShapeDtypeStruct((m_pad, d), BF16),
        compiler_params=_cparams("arbitrary"),
    )(src, piece_used, h)


def _group_swiglu_kernel(te_ref, tv_ref, a_ref, wg_ref, wu_ref, o_ref, wgb_ref, wub_ref):
    i = pl.program_id(1)
    new_expert = jnp.logical_or(i == 0, te_ref[i] != te_ref[jnp.maximum(i - 1, 0)])

    @pl.when(new_expert)
    def _():
        wgb_ref[...] = wg_ref[...].astype(BF16)
        wub_ref[...] = wu_ref[...].astype(BF16)

    tm = a_ref.shape[0]
    n_steps = tv_ref[i]

    def compute(n_rows):
        a = a_ref[:n_rows, :]
        g = _dot(a, wgb_ref[...])
        u = _dot(a, wub_ref[...])
        o_ref[:n_rows, :] = (_silu(g) * u).astype(o_ref.dtype)
        if n_rows < tm:
            o_ref[n_rows:, :] = jnp.zeros((tm - n_rows, o_ref.shape[1]), o_ref.dtype)

    for s in range(1, tm // MOE_ROW_STEP + 1):
        pl.when(n_steps == s)(functools.partial(compute, s * MOE_ROW_STEP))

    @pl.when(n_steps == 0)
    def _():
        o_ref[...] = jnp.zeros_like(o_ref)


def _group_swiglu(xs, w_gu, layer, tile_expert, tile_valid, tm, tn=512):
    m_pad, k = xs.shape
    f = w_gu.shape[-1] // 2
    nt = f // tn
    return pl.pallas_call(
        _group_swiglu_kernel,
        grid_spec=pltpu.PrefetchScalarGridSpec(
            num_scalar_prefetch=2,
            grid=(nt, m_pad // tm),
            in_specs=[pl.BlockSpec((tm, k), lambda n, i, te, tv: (jnp.where(tv[i] > 0, i, 0), 0)),
                      pl.BlockSpec((None, None, k, tn), lambda n, i, te, tv: (layer, te[i], 0, n)),
                      pl.BlockSpec((None, None, k, tn), lambda n, i, te, tv: (layer, te[i], 0, nt + n))],
            out_specs=pl.BlockSpec((tm, tn), lambda n, i, te, tv: (i, n)),
            scratch_shapes=[pltpu.VMEM((k, tn), BF16), pltpu.VMEM((k, tn), BF16)]),
        out_shape=jax.ShapeDtypeStruct((m_pad, f), BF16),
        compiler_params=_cparams("arbitrary", "arbitrary"),
    )(tile_expert, tile_valid, xs, w_gu, w_gu)


def _group_down_kernel(te_ref, tv_ref, a_ref, w_ref, o_ref, *, nk):
    i = pl.program_id(0)
    kk = pl.program_id(1)

    @pl.when(kk == 0)
    def _():
        o_ref[...] = jnp.zeros_like(o_ref)

    n_steps = tv_ref[i]

    def accumulate(n_rows):
        o_ref[:n_rows, :] += _dot(a_ref[:n_rows, :], w_ref[...].astype(BF16))

    for s in range(1, a_ref.shape[0] // MOE_ROW_STEP + 1):
        pl.when(n_steps == s)(functools.partial(accumulate, s * MOE_ROW_STEP))


def _group_down(hm, w_down, tile_expert, tile_valid, tm, tk=1024):
    m_pad, f = hm.shape
    d = w_down.shape[-1]
    nk = f // tk
    return pl.pallas_call(
        functools.partial(_group_down_kernel, nk=nk),
        grid_spec=pltpu.PrefetchScalarGridSpec(
            num_scalar_prefetch=2,
            grid=(m_pad // tm, nk),
            in_specs=[pl.BlockSpec((tm, tk), lambda i, kk, te, tv: (i, jnp.where(tv[i] > 0, kk, 0))),
                      pl.BlockSpec((None, tk, d), lambda i, kk, te, tv: (te[i], jnp.where(tv[i] > 0, kk, 0), 0))],
            out_specs=pl.BlockSpec((tm, d), lambda i, kk, te, tv: (i, 0))),
        out_shape=jax.ShapeDtypeStruct((m_pad, d), F32),
        compiler_params=_cparams("arbitrary", "arbitrary"),
    )(tile_expert, tile_valid, hm, w_down)


def _combine_ln_kernel(d1_ref, d2_ref, o_hbm, wt_ref, x_ref, gate_ref, g_ref, b_ref, xo_ref, buf1, buf2, sem, *, tm):
    base = pl.program_id(0) * tm

    def start(r, carry):
        _row_copy(o_hbm, d1_ref[base + r], buf1, r, sem).start()
        _row_copy(o_hbm, d2_ref[base + r], buf2, r, sem).start()
        return carry

    lax.fori_loop(0, tm, start, 0, unroll=8)
    pltpu.make_async_copy(o_hbm.at[pl.ds(0, tm)], buf1, sem).wait()
    pltpu.make_async_copy(o_hbm.at[pl.ds(0, tm)], buf2, sem).wait()
    wt = wt_ref[...]
    y = wt[:, 0:1] * buf1[...] + wt[:, 1:2] * buf2[...]
    r = DN_ALPHA * x_ref[...] + gate_ref[0] * y
    xo_ref[...] = _layer_norm(r, g_ref[...], b_ref[...])


def _combine_ln(o_sorted, d1, d2, wts, xf, gate, ln_g, ln_b, seq, tm=256):
    n, d = xf.shape
    row = lambda i, d1, d2: (i, 0)
    const = lambda i, d1, d2: (0, 0)
    return pl.pallas_call(
        functools.partial(_combine_ln_kernel, tm=tm),
        grid_spec=pltpu.PrefetchScalarGridSpec(
            num_scalar_prefetch=2,
            grid=(n // tm,),
            in_specs=[pl.BlockSpec(memory_space=pl.ANY),
                      pl.BlockSpec((tm, LANES), row),
                      pl.BlockSpec((tm, d), row),
                      pl.BlockSpec((1, 1, d), lambda i, d1, d2: (i * tm // seq, 0, 0)),
                      pl.BlockSpec((1, d), const),
                      pl.BlockSpec((1, d), const)],
            out_specs=pl.BlockSpec((tm, d), row),
            scratch_shapes=[pltpu.VMEM((tm, d), F32), pltpu.VMEM((tm, d), F32), pltpu.SemaphoreType.DMA(())]),
        out_shape=jax.ShapeDtypeStruct((n, d), F32),
        compiler_params=_cparams("arbitrary"),
    )(d1, d2, o_sorted, wts, xf, gate, ln_g.reshape(1, d), ln_b.reshape(1, d))


def kernel(x, c, w_ada, b_ada, ln_g, ln_b, gdn_w_in, gdn_conv_w, gdn_a_log, gdn_dt_bias, gdn_norm_w, gdn_w_out,
           kv_w_ada, kv_b_ada, sb_w_kv, sb_w_q, sb_w_out, ffn_w_gu, ffn_w_down, moe_w_router, moe_b_router,
           moe_w_gu, moe_w_down):
    bsz, seq, d = x.shape
    n = bsz * seq
    assert w_ada.shape[0] == DEPTH and gdn_w_in.shape[0] == 1 and sb_w_q.shape[0] == 1
    n_vh = gdn_a_log.shape[1]
    n_qk = n_vh // 2
    key_dim = n_qk * HEAD_DIM
    value_dim = n_vh * HEAD_DIM
    main_cols = 2 * key_dim + 2 * value_dim
    n_sb_heads = sb_w_q.shape[-1] // HEAD_DIM
    xf = x.reshape(n, d)

    c_pad = jnp.zeros((8, d), F32).at[:bsz].set(c)
    def mod_vectors(w, layer, b, parts):
        m = _ada(c_pad, w, layer, b)[:bsz]
        return [v.reshape(bsz, 1, d) for v in jnp.split(m, parts, axis=-1)]
    sh_m0, sc_m0, gt_m0, sh_f0, sc_f0, gt_f0 = mod_vectors(w_ada, 0, b_ada[0], 6)
    sh_m1, sc_m1, gt_m1, sh_f1, sc_f1, gt_f1 = mod_vectors(w_ada, 1, b_ada[1], 6)
    sh_kv, sc_kv = mod_vectors(kv_w_ada[None], 0, kv_b_ada, 2)

    h = _modulate(xf, sh_m0, sc_m0, seq)
    proj = _ws_mm(h, gdn_w_in, 0, main_cols, F32)
    w_gate = jnp.zeros((1, d, LANES), F32).at[0, :, :2 * n_vh].set(gdn_w_in[0, :, main_cols:])
    gate_logits = _ws_mm(h, w_gate, 0, LANES, F32)
    prm = jnp.zeros((8, LANES), F32)
    prm = prm.at[0, n_vh:2 * n_vh].set(gdn_dt_bias[0].astype(F32))
    prm = prm.at[1, n_vh:2 * n_vh].set(-jnp.exp(gdn_a_log[0].astype(F32)))
    conv_wt = jnp.transpose(gdn_conv_w[0])
    lg3 = gate_logits.reshape(bsz, seq, LANES)
    lgt = jnp.transpose(lg3[:, :, n_vh:2 * n_vh], (0, 2, 1)).reshape(bsz, n_vh, seq // GDN_CHUNK, GDN_CHUNK)
    prow = jnp.broadcast_to(prm[0:2, n_vh:2 * n_vh].T[:, :, None], (n_vh, 2, GDN_CHUNK))
    o = _gdn(proj.reshape(bsz, seq, main_cols), lg3, lgt, prm, prow, conv_wt,
             gdn_norm_w[0].reshape(1, HEAD_DIM), n_qk)
    x1, h1 = _proj_ln(o.reshape(n, value_dim), gdn_w_out[0].astype(BF16), xf, gt_m0, ln_g[0, 0], ln_b[0, 0],
                      [(sh_f0, sc_f0)], seq)

    hm = _ws_swiglu(h1, ffn_w_gu, 0)
    x2, h2, h_kv = _proj_ln(hm, ffn_w_down[0].astype(BF16), x1, gt_f0, ln_g[0, 1], ln_b[0, 1],
                            [(sh_m1, sc_m1), (sh_kv, sc_kv)], seq)

    kv = _ws_mm(h_kv, sb_w_kv[None], 0, 2 * n_sb_heads * HEAD_DIM, BF16)
    q = _ws_mm(h2, sb_w_q, 0, n_sb_heads * HEAD_DIM, BF16)
    oa = _sb_attention(q.reshape(bsz, seq, -1), kv.reshape(bsz, seq, -1), n_sb_heads)
    w_router = jnp.zeros((d, LANES), F32).at[:, :N_EXPERTS].set(moe_w_router[0])
    x3, h3, logits = _proj_ln(oa.reshape(n, -1), sb_w_out[0].astype(BF16), x2, gt_m1, ln_g[1, 0], ln_b[1, 0],
                              [(sh_f1, sc_f1)], seq, h_dtype=F32, w_router=w_router)

    b_router = jnp.zeros((1, LANES), F32).at[0, :N_EXPERTS].set(moe_b_router[0].astype(F32))
    idx, wts, cnt = _route(logits, b_router)
    tm_g = MOE_TILE
    n_tiles = 2 * n // tm_g + N_EXPERTS
    counts = cnt[0, :N_EXPERTS].astype(jnp.int32)
    tiles_per = (counts + tm_g - 1) // tm_g
    tile_end = jnp.cumsum(tiles_per)
    tile_first = tile_end - tiles_per
    row_off = tile_first * tm_g
    tile_ids = jnp.arange(n_tiles, dtype=jnp.int32)
    tile_expert = jnp.minimum(jnp.searchsorted(tile_end, tile_ids, side="right"), N_EXPERTS - 1).astype(jnp.int32)
    rows_left = counts[tile_expert] - (tile_ids - tile_first[tile_expert]) * tm_g
    tile_rows = jnp.where(tile_ids < tile_end[-1], jnp.clip(rows_left, 0, tm_g), 0)
    tile_steps = ((tile_rows + MOE_ROW_STEP - 1) // MOE_ROW_STEP).astype(jnp.int32)
    pieces = tm_g // MOE_ROW_STEP
    piece_used = (jnp.arange(pieces, dtype=jnp.int32)[None, :] < tile_steps[:, None]).astype(jnp.int32).reshape(-1)
    dest1 = row_off[idx[:, 0]] + idx[:, 2]
    dest2 = row_off[idx[:, 1]] + idx[:, 3]
    tok = jnp.arange(n, dtype=jnp.int32)
    src = jnp.zeros((n_tiles * tm_g,), jnp.int32).at[dest1].set(tok).at[dest2].set(tok)

    xs = _gather_rows(h3, src, piece_used)
    hm2 = _group_swiglu(xs, moe_w_gu, 0, tile_expert, tile_steps, tm_g)
    o_sorted = _group_down(hm2, moe_w_down[0], tile_expert, tile_steps, tm_g)
    out = _combine_ln(o_sorted, dest1, dest2, wts, x3, gt_f1, ln_g[1, 1], ln_b[1, 1], seq)
    return out.reshape(bsz, seq, d)
```

```python
import functools

import jax
import jax.numpy as jnp
from jax import lax
from jax.experimental import pallas as pl
from jax.experimental.pallas import tpu as pltpu

F32 = jnp.float32
BF16 = jnp.bfloat16

LANES = 128
VMEM_LIMIT_BYTES = 56 * 1024 * 1024

HEAD_DIM = 128
GDN_CHUNK = 64
CONV_WIDTH = 4
N_EXPERTS = 8
MOE_TILE = 1024
MOE_ROW_STEP = 256
LN_EPS = 1e-5
RMS_EPS = 1e-6
L2_EPS = 1e-6
DEPTH = 2
DN_ALPHA = (2.0 * DEPTH) ** 0.25


def _cparams(*sem):
    return pltpu.CompilerParams(dimension_semantics=sem, vmem_limit_bytes=VMEM_LIMIT_BYTES)


def _dot(a, b):
    return jnp.dot(a, b, preferred_element_type=F32)


def _dot_nt(a, b):
    return lax.dot_general(a, b, (((1,), (1,)), ((), ())), preferred_element_type=F32)


def _dot_tn(a, b):
    return lax.dot_general(a, b, (((0,), (0,)), ((), ())), preferred_element_type=F32)


def _split(a):
    hi = a.astype(BF16)
    lo = (a - hi.astype(F32)).astype(BF16)
    return hi, lo


def _dot3(a, b):
    ah, al = _split(a)
    bh, bl = _split(b)
    return _dot(ah, bh) + (_dot(ah, bl) + _dot(al, bh))


def _softplus(x):
    return jnp.maximum(x, 0.0) + jnp.log(1.0 + jnp.exp(-jnp.abs(x)))


def _silu(x):
    return x * jax.nn.sigmoid(x)


def _ada_kernel(c_ref, w_ref, b_ref, o_ref):
    s = _silu(c_ref[...]).astype(BF16)
    o_ref[...] = _dot(s, w_ref[...].astype(BF16)) + b_ref[...]


def _ada(c_pad, w, layer, b, tn=1024):
    _, d, n_out = w.shape
    return pl.pallas_call(
        _ada_kernel,
        grid=(n_out // tn,),
        in_specs=[pl.BlockSpec((8, d), lambda n: (0, 0)),
                  pl.BlockSpec((None, d, tn), lambda n: (layer, 0, n)),
                  pl.BlockSpec((1, tn), lambda n: (0, n))],
        out_specs=pl.BlockSpec((8, tn), lambda n: (0, n)),
        out_shape=jax.ShapeDtypeStruct((8, n_out), F32),
        compiler_params=_cparams("arbitrary"),
    )(c_pad, w, b.reshape(1, n_out))


def _modulate_kernel(x_ref, sh_ref, sc_ref, o_ref):
    o_ref[...] = (x_ref[...] * (1.0 + sc_ref[0]) + sh_ref[0]).astype(o_ref.dtype)


def _modulate(xf, shift, scale, seq, tm=512):
    n, d = xf.shape
    bmap = lambda i: (i * tm // seq, 0, 0)
    return pl.pallas_call(
        _modulate_kernel,
        grid=(n // tm,),
        in_specs=[pl.BlockSpec((tm, d), lambda i: (i, 0)),
                  pl.BlockSpec((1, 1, d), bmap),
                  pl.BlockSpec((1, 1, d), bmap)],
        out_specs=pl.BlockSpec((tm, d), lambda i: (i, 0)),
        out_shape=jax.ShapeDtypeStruct((n, d), BF16),
        compiler_params=_cparams("arbitrary"),
    )(xf, shift, scale)


def _ws_mm_kernel(a_ref, w_ref, o_ref, wb_ref, *, transposed):
    @pl.when(pl.program_id(1) == 0)
    def _():
        w = w_ref[...]
        wb_ref[...] = (w.T if transposed else w).astype(BF16)

    o_ref[...] = _dot(a_ref[...], wb_ref[...]).astype(o_ref.dtype)


def _ws_mm(a, w, layer, n_out, out_dtype, transposed=False, tm=1024, tn=1024):
    m, k = a.shape
    tn = min(tn, n_out)
    if transposed:
        w_spec = pl.BlockSpec((None, tn, k), lambda n, i: (layer, n, 0))
    else:
        w_spec = pl.BlockSpec((None, k, tn), lambda n, i: (layer, 0, n))
    return pl.pallas_call(
        functools.partial(_ws_mm_kernel, transposed=transposed),
        grid=(n_out // tn, m // tm),
        in_specs=[pl.BlockSpec((tm, k), lambda n, i: (i, 0)), w_spec],
        out_specs=pl.BlockSpec((tm, tn), lambda n, i: (i, n)),
        out_shape=jax.ShapeDtypeStruct((m, n_out), out_dtype),
        scratch_shapes=[pltpu.VMEM((k, tn), BF16)],
        compiler_params=_cparams("arbitrary", "arbitrary"),
    )(a, w)


def _narrow_mm_kernel(a_ref, w_hbm, o_ref, wf_ref, wb_ref, sem, *, layer, row0, n_rows):
    @pl.when(pl.program_id(0) == 0)
    def _():
        copy = pltpu.make_async_copy(w_hbm.at[layer, pl.ds(row0, n_rows), :], wf_ref, sem)
        copy.start()
        copy.wait()
        wb_ref[...] = jnp.zeros_like(wb_ref)
        wb_ref[:n_rows, :] = wf_ref[...].astype(BF16)

    o_ref[...] = _dot_nt(a_ref[...], wb_ref[...])


def _narrow_mm(a, w_t, layer, row0, n_rows, tm=1024):
    m, k = a.shape
    return pl.pallas_call(
        functools.partial(_narrow_mm_kernel, layer=layer, row0=row0, n_rows=n_rows),
        grid=(m // tm,),
        in_specs=[pl.BlockSpec((tm, k), lambda i: (i, 0)),
                  pl.BlockSpec(memory_space=pl.ANY)],
        out_specs=pl.BlockSpec((tm, LANES), lambda i: (i, 0)),
        out_shape=jax.ShapeDtypeStruct((m, LANES), F32),
        scratch_shapes=[pltpu.VMEM((n_rows, k), F32), pltpu.VMEM((LANES, k), BF16), pltpu.SemaphoreType.DMA(())],
        compiler_params=_cparams("arbitrary"),
    )(a, w_t)


def _ws_swiglu_kernel(a_ref, wg_ref, wu_ref, o_ref, wgb_ref, wub_ref):
    @pl.when(pl.program_id(1) == 0)
    def _():
        wgb_ref[...] = wg_ref[...].astype(BF16)
        wub_ref[...] = wu_ref[...].astype(BF16)

    a = a_ref[...]
    g = _dot(a, wgb_ref[...])
    u = _dot(a, wub_ref[...])
    o_ref[...] = (_silu(g) * u).astype(o_ref.dtype)


def _ws_swiglu(a, w_gu, layer, tm=1024, tn=512):
    m, k = a.shape
    f = w_gu.shape[-1] // 2
    nt = f // tn
    return pl.pallas_call(
        _ws_swiglu_kernel,
        grid=(nt, m // tm),
        in_specs=[pl.BlockSpec((tm, k), lambda n, i: (i, 0)),
                  pl.BlockSpec((None, k, tn), lambda n, i: (layer, 0, n)),
                  pl.BlockSpec((None, k, tn), lambda n, i: (layer, 0, nt + n))],
        out_specs=pl.BlockSpec((tm, tn), lambda n, i: (i, n)),
        out_shape=jax.ShapeDtypeStruct((m, f), BF16),
        scratch_shapes=[pltpu.VMEM((k, tn), BF16), pltpu.VMEM((k, tn), BF16)],
        compiler_params=_cparams("arbitrary", "arbitrary"),
    )(a, w_gu, w_gu)


def _layer_norm(r, g, b):
    mu = jnp.mean(r, axis=-1, keepdims=True)
    rc = r - mu
    var = jnp.mean(rc * rc, axis=-1, keepdims=True)
    return rc * lax.rsqrt(var + LN_EPS) * g + b


def _proj_ln_kernel(*refs, n_mod, nk, with_router):
    a_ref, w_ref, x_ref, gate_ref, g_ref, b_ref = refs[:6]
    mod_refs = refs[6:6 + 2 * n_mod]
    pos = 6 + 2 * n_mod
    wr_ref = refs[pos] if with_router else None
    pos += int(with_router)
    xo_ref = refs[pos]
    h_refs = refs[pos + 1:pos + 1 + n_mod]
    pos += 1 + n_mod
    lg_ref = refs[pos] if with_router else None
    acc_ref = refs[-1]
    kk = pl.program_id(1)

    @pl.when(kk == 0)
    def _():
        acc_ref[...] = jnp.zeros_like(acc_ref)

    acc_ref[...] += _dot(a_ref[...], w_ref[...])

    @pl.when(kk == nk - 1)
    def _():
        r = DN_ALPHA * x_ref[...] + gate_ref[0] * acc_ref[...]
        xn = _layer_norm(r, g_ref[...], b_ref[...])
        xo_ref[...] = xn
        for t in range(n_mod):
            h = xn * (1.0 + mod_refs[2 * t + 1][0]) + mod_refs[2 * t][0]
            h_refs[t][...] = h.astype(h_refs[t].dtype)
            if with_router and t == 0:
                lg_ref[...] = _dot3(h, wr_ref[...])


def _k_tile(k, limit=2048, unit=256):
    nk = 1
    while k % nk or k // nk > limit or (k // nk) % unit:
        nk += 1
    return k // nk


def _proj_ln(a, w, xf, gate, ln_g, ln_b, mods, seq, h_dtype=BF16, w_router=None, tm=512):
    m, k = a.shape
    d = w.shape[-1]
    tk = _k_tile(k)
    nk = k // tk
    n_mod = len(mods)
    with_router = w_router is not None
    bmap = lambda i, kk: (i * tm // seq, 0, 0)
    row = lambda i, kk: (i, 0)
    const = lambda i, kk: (0, 0)
    in_specs = [pl.BlockSpec((tm, tk), lambda i, kk: (i, kk)),
                pl.BlockSpec((tk, d), lambda i, kk: (kk, 0)),
                pl.BlockSpec((tm, d), row),
                pl.BlockSpec((1, 1, d), bmap),
                pl.BlockSpec((1, d), const),
                pl.BlockSpec((1, d), const)]
    args = [a, w, xf, gate, ln_g.reshape(1, d), ln_b.reshape(1, d)]
    for sh, sc in mods:
        in_specs += [pl.BlockSpec((1, 1, d), bmap), pl.BlockSpec((1, 1, d), bmap)]
        args += [sh, sc]
    out_specs = [pl.BlockSpec((tm, d), row)] + [pl.BlockSpec((tm, d), row)] * n_mod
    out_shape = [jax.ShapeDtypeStruct((m, d), F32)] + [jax.ShapeDtypeStruct((m, d), h_dtype)] * n_mod
    if with_router:
        in_specs.append(pl.BlockSpec((d, LANES), const))
        args.append(w_router)
        out_specs.append(pl.BlockSpec((tm, LANES), row))
        out_shape.append(jax.ShapeDtypeStruct((m, LANES), F32))
    return pl.pallas_call(
        functools.partial(_proj_ln_kernel, n_mod=n_mod, nk=nk, with_router=with_router),
        grid=(m // tm, nk),
        in_specs=in_specs,
        out_specs=out_specs,
        out_shape=out_shape,
        scratch_shapes=[pltpu.VMEM((tm, d), F32)],
        compiler_params=_cparams("arbitrary", "arbitrary"),
    )(*args)


GDN_GROUP = 4


def _gdn_kernel(q_ref, k_ref, v_ref, z_ref, cwq_ref, cwk_ref, cwv_ref, lg_ref, lgt_ref, prm_ref, prow_ref, nw_ref,
                o_ref, qn_s, kn_s, vc_s, ga_s, gcr_s, kw_r, nn_r, qp_r, op_r, dg_r):
    j = pl.program_id(1)
    seq = q_ref.shape[1]
    c = GDN_CHUNK
    hd = HEAD_DIM
    n_groups = seq // (c * GDN_GROUP)
    n_vh = 2 * pl.num_programs(1)
    row = lax.broadcasted_iota(jnp.int32, (seq, 1), 0)

    def conv_silu(x, cw):
        acc = x * cw[CONV_WIDTH - 1:CONV_WIDTH, :]
        for s in range(1, CONV_WIDTH):
            xs = jnp.where(row >= s, pltpu.roll(x, s, axis=0), 0.0)
            acc = acc + xs * cw[CONV_WIDTH - 1 - s:CONV_WIDTH - s, :]
        return _silu(acc)

    def l2norm(x):
        return x * lax.rsqrt(jnp.sum(x * x, axis=-1, keepdims=True) + L2_EPS)

    qn_s[...] = l2norm(conv_silu(q_ref[0], cwq_ref[...])) * (hd ** -0.5)
    kn_s[...] = l2norm(conv_silu(k_ref[0], cwk_ref[...]))
    vc_s[...] = conv_silu(v_ref[0], cwv_ref[...])
    ga_s[...] = prm_ref[1:2, :] * _softplus(lg_ref[0] + prm_ref[0:1, :])

    ri = lax.broadcasted_iota(jnp.int32, (c, c), 0)
    ci = lax.broadcasted_iota(jnp.int32, (c, c), 1)
    upper = (ri <= ci).astype(BF16)
    for hh in range(2):
        g_row = prow_ref[hh, 1:2, :] * _softplus(lgt_ref[0, hh] + prow_ref[hh, 0:1, :])
        g1, g2 = _split(g_row)
        g3 = (g_row - g1.astype(F32) - g2.astype(F32)).astype(BF16)
        gcr_s[hh] = _dot(g1, upper) + (_dot(g2, upper) + _dot(g3, upper))

    incl = ri >= ci
    strict = ri > ci
    lane = lax.broadcasted_iota(jnp.int32, (c, LANES), 1)
    pos = lax.broadcasted_iota(jnp.int32, (c, 1), 0)

    def select_lane(x, idx):
        return jnp.sum(jnp.where(lane == idx, x, 0.0), axis=1, keepdims=True)

    def prepare_group(first_chunk, slot):
        chunks = []
        for k in range(GDN_GROUP):
            rows = pl.ds(pl.multiple_of((first_chunk + k) * c, c), c)
            kc = kn_s[rows, :]
            qc = qn_s[rows, :]
            chunks.append((first_chunk + k, rows, kc, qc, kc.astype(BF16), qc.astype(BF16)))
        kks = [_dot_nt(kcb, kcb) for (_, _, _, _, kcb, _) in chunks]
        qks = [_dot_nt(qcb, kcb) for (_, _, _, _, kcb, qcb) in chunks]
        probs = []
        for (ch, rows, kc, qc, _, _), kk, qk in zip(chunks, kks, qks):
            lgc = lg_ref[0, rows, :]
            gac = ga_s[rows, :]
            for hh in range(2):
                head = 2 * j + hh
                beta = jax.nn.sigmoid(select_lane(lgc, head))
                gc = jnp.broadcast_to(select_lane(gac, n_vh + head), (c, hd))
                s = 1
                while s < c:
                    gc = gc + jnp.where(pos >= s, pltpu.roll(gc, s, axis=0), 0.0)
                    s *= 2
                gl = jnp.broadcast_to(gc[c - 1:c, :], (c, hd))
                eg = jnp.exp(gc)
                row_b = jnp.broadcast_to(gcr_s[hh, pl.ds(ch, 1), :], (c, c))
                dec = jnp.exp(jnp.where(incl, gc[:, :c] - row_b, -jnp.inf))
                low = jnp.where(strict, kk * beta * dec, 0.0)
                rhs = jnp.concatenate([vc_s[rows, hh * hd:(hh + 1) * hd] * beta, kc * beta * eg], axis=1)
                probs.append(dict(low=low, rhs=rhs, at=(qk * dec).astype(BF16), qd=qc * eg,
                                  kd=(kc * jnp.exp(gl - gc)).astype(BF16), dg=jnp.exp(gl[:8, :])))
        yield
        eye = (ri == ci).astype(F32)
        ps = [p["low"] for p in probs]
        xs = [eye - p["low"] for p in probs]
        n_terms = 2
        while n_terms < c:
            pbs = [p.astype(BF16) for p in ps]
            ps = [_dot(pb, pb) for pb in pbs]
            xs = [x + _dot(x.astype(BF16), p.astype(BF16)) for x, p in zip(xs, ps)]
            n_terms *= 2
            if n_terms in (4, 16, 64):
                yield
        sols = [_dot(x.astype(BF16), p["rhs"].astype(BF16)) for x, p in zip(xs, probs)]
        ubs = [sol[:, :hd].astype(BF16) for sol in sols]
        wbs = [sol[:, hd:].astype(BF16) for sol in sols]
        kws = [_dot_tn(p["kd"], wb) for p, wb in zip(probs, wbs)]
        nns = [_dot_tn(p["kd"], ub) for p, ub in zip(probs, ubs)]
        aws = [_dot(p["at"], wb) for p, wb in zip(probs, wbs)]
        ops = [_dot(p["at"], ub) for p, ub in zip(probs, ubs)]
        for n, p in enumerate(probs):
            k, hh = divmod(n, 2)
            kw_r[slot, k, hh] = kws[n].astype(BF16)
            nn_r[slot, k, hh] = nns[n]
            qp_r[slot, k, hh] = (p["qd"] - aws[n]).astype(BF16)
            op_r[slot, k, hh] = ops[n]
            dg_r[slot, k, hh] = p["dg"]

    def scan(ch, slot, k, states):
        rows = pl.ds(pl.multiple_of(ch * c, c), c)
        new_states = []
        for hh in range(2):
            st = states[hh]
            sb = st.astype(BF16)
            o = _dot(qp_r[slot, k, hh], sb) + op_r[slot, k, hh]
            decay = jnp.broadcast_to(dg_r[slot, k, hh][0:1, :], (hd, hd))
            new_states.append(st * decay - _dot(kw_r[slot, k, hh], sb) + nn_r[slot, k, hh])
            o = o * lax.rsqrt(jnp.mean(o * o, axis=-1, keepdims=True) + RMS_EPS) * nw_ref[...]
            cols = slice(hh * hd, (hh + 1) * hd)
            o_ref[0, rows, cols] = (o * _silu(z_ref[0, rows, cols])).astype(o_ref.dtype)
        return tuple(new_states)

    for _ in prepare_group(0, 0):
        pass

    def body(g, states):
        slot = g % 2
        stages = prepare_group((g + 1) * GDN_GROUP, 1 - slot)
        for k in range(GDN_GROUP):
            next(stages)
            states = scan(g * GDN_GROUP + k, slot, k, states)
        for _ in stages:
            pass
        return states

    zero = jnp.zeros((hd, hd), F32)
    states = lax.fori_loop(0, n_groups - 1, body, (zero, zero))
    last = n_groups - 1
    for k in range(GDN_GROUP):
        states = scan(last * GDN_GROUP + k, last % 2, k, states)


def _gdn(proj, lg, lgt, prm, prow, conv_wt, norm_w, n_qk_heads):
    bsz, seq, _ = proj.shape
    hq = n_qk_heads
    hd = HEAD_DIM
    n_chunks = seq // GDN_CHUNK
    ring = (2, GDN_GROUP, 2)
    return pl.pallas_call(
        _gdn_kernel,
        grid=(bsz, hq),
        in_specs=[pl.BlockSpec((1, seq, hd), lambda b, j: (b, 0, j)),
                  pl.BlockSpec((1, seq, hd), lambda b, j: (b, 0, hq + j)),
                  pl.BlockSpec((1, seq, 2 * hd), lambda b, j: (b, 0, hq + j)),
                  pl.BlockSpec((1, seq, 2 * hd), lambda b, j: (b, 0, 2 * hq + j)),
                  pl.BlockSpec((CONV_WIDTH, hd), lambda b, j: (0, j)),
                  pl.BlockSpec((CONV_WIDTH, hd), lambda b, j: (0, hq + j)),
                  pl.BlockSpec((CONV_WIDTH, 2 * hd), lambda b, j: (0, hq + j)),
                  pl.BlockSpec((1, seq, LANES), lambda b, j: (b, 0, 0)),
                  pl.BlockSpec((1, 2, n_chunks, GDN_CHUNK), lambda b, j: (b, j, 0, 0)),
                  pl.BlockSpec((8, LANES), lambda b, j: (0, 0)),
                  pl.BlockSpec((2, 2, GDN_CHUNK), lambda b, j: (j, 0, 0)),
                  pl.BlockSpec((1, hd), lambda b, j: (0, 0))],
        out_specs=pl.BlockSpec((1, seq, 2 * hd), lambda b, j: (b, 0, j)),
        out_shape=jax.ShapeDtypeStruct((bsz, seq, 2 * hq * hd), BF16),
        scratch_shapes=[pltpu.VMEM((seq, hd), F32), pltpu.VMEM((seq, hd), F32),
                        pltpu.VMEM((seq, 2 * hd), F32),
                        pltpu.VMEM((seq, LANES), F32),
                        pltpu.VMEM((2, n_chunks, GDN_CHUNK), F32),
                        pltpu.VMEM(ring + (hd, hd), BF16),
                        pltpu.VMEM(ring + (hd, hd), F32),
                        pltpu.VMEM(ring + (GDN_CHUNK, hd), BF16),
                        pltpu.VMEM(ring + (GDN_CHUNK, hd), F32),
                        pltpu.VMEM(ring + (8, hd), F32)],
        compiler_params=_cparams("arbitrary", "arbitrary"),
    )(proj, proj, proj, proj, conv_wt, conv_wt, conv_wt, lg, lgt, prm, prow, norm_w)


SB_HEADS_PER_STEP = 4
SB_ZERO_EXPONENT = -110.0


def _sb_kernel(q_ref, k_ref, v_ref, o_ref, kmax_s, *, blk):
    i = pl.program_id(2)
    hd = HEAD_DIM
    scale = hd ** -0.5
    heads = range(SB_HEADS_PER_STEP)

    @pl.when(i == 0)
    def _():
        for hh in heads:
            kf = k_ref[0, :, hh * hd:(hh + 1) * hd].astype(F32)
            n2 = jnp.max(jnp.sum(kf * kf, axis=1, keepdims=True), axis=0, keepdims=True)
            kmax_s[hh] = jnp.broadcast_to(jnp.sqrt(n2) * scale, kmax_s.shape[1:])

    ri = lax.broadcasted_iota(jnp.int32, (blk, blk), 0)
    ci = lax.broadcasted_iota(jnp.int32, (blk, blk), 1)
    tri = (ri >= ci).astype(BF16)
    causal = ci < ri
    qs = [q_ref[0, :, hh * hd:(hh + 1) * hd] for hh in heads]
    z_bound = []
    for hh in heads:
        qf = qs[hh].astype(F32)
        z_bound.append(jnp.sqrt(jnp.sum(qf * qf, axis=1, keepdims=True)) * kmax_s[hh][0:1, 0:1])

    def block(jb, accs, runs, diagonal):
        rows = pl.ds(pl.multiple_of(jb * blk, blk), blk)
        cols = [slice(hh * hd, (hh + 1) * hd) for hh in heads]
        zs = [_dot_nt(qs[hh], k_ref[0, rows, cols[hh]]) * scale for hh in heads]
        lss = [-_softplus(z) for z in zs]
        if diagonal:
            lss = [jnp.where(causal, ls, 0.0) for ls in lss]
        parts = [_split(ls) for ls in lss]
        rcs = [_dot(hi, tri) + _dot(lo, tri) for hi, lo in parts]
        ws = [jnp.exp(zs[hh] + rcs[hh] + runs[hh]) for hh in heads]
        if diagonal:
            ws = [jnp.where(causal, w, 0.0) for w in ws]
        new_accs = [accs[hh] + _dot(ws[hh].astype(BF16), v_ref[0, rows, cols[hh]]) for hh in heads]
        new_runs = [runs[hh] + rcs[hh][:, :1] for hh in heads]
        return tuple(new_accs), tuple(new_runs)

    def any_weight_left(runs):
        m = jnp.max(runs[0] + z_bound[0])
        for hh in heads[1:]:
            m = jnp.maximum(m, jnp.max(runs[hh] + z_bound[hh]))
        return (m > SB_ZERO_EXPONENT).astype(jnp.int32)

    zeros = tuple(jnp.zeros((blk, hd), F32) for _ in heads)
    zrun = tuple(jnp.zeros((blk, 1), F32) for _ in heads)
    accs, runs = block(i, zeros, zrun, True)

    def cond(carry):
        return jnp.logical_and(carry[0] >= 0, carry[1] > 0)

    def body(carry):
        jb, _, accs, runs = carry
        accs, runs = block(jb, accs, runs, False)
        return jb - 1, any_weight_left(runs), accs, runs

    _, _, accs, _ = lax.while_loop(cond, body, (i - 1, any_weight_left(runs), accs, runs))
    for hh in heads:
        o_ref[0, :, hh * hd:(hh + 1) * hd] = accs[hh].astype(o_ref.dtype)


def _sb_attention(q, kv, n_heads, blk=256):
    bsz, seq, _ = q.shape
    w = SB_HEADS_PER_STEP * HEAD_DIM
    n_hp = n_heads // SB_HEADS_PER_STEP
    return pl.pallas_call(
        functools.partial(_sb_kernel, blk=blk),
        grid=(bsz, n_hp, seq // blk),
        in_specs=[pl.BlockSpec((1, blk, w), lambda b, h, i: (b, i, h)),
                  pl.BlockSpec((1, seq, w), lambda b, h, i: (b, 0, h)),
                  pl.BlockSpec((1, seq, w), lambda b, h, i: (b, 0, n_hp + h))],
        out_specs=pl.BlockSpec((1, blk, w), lambda b, h, i: (b, i, h)),
        out_shape=jax.ShapeDtypeStruct((bsz, seq, n_heads * HEAD_DIM), BF16),
        scratch_shapes=[pltpu.VMEM((SB_HEADS_PER_STEP, 8, LANES), F32)],
        compiler_params=_cparams("arbitrary", "arbitrary", "arbitrary"),
    )(q, kv, kv)


def _route_kernel(lg_ref, br_ref, idx_ref, wt_ref, cnt_ref, carry_ref, *, tm):
    i = pl.program_id(0)

    @pl.when(i == 0)
    def _():
        carry_ref[...] = jnp.zeros_like(carry_ref)

    lane = lax.broadcasted_iota(jnp.int32, (tm, LANES), 1)
    lane_f = lane.astype(F32)
    logits = jnp.where(lane < N_EXPERTS, lg_ref[...] + br_ref[...], -jnp.inf)
    m1 = jnp.max(logits, axis=1, keepdims=True)
    i1 = jnp.min(jnp.where(logits == m1, lane_f, float(LANES)), axis=1, keepdims=True)
    oh1 = lane_f == i1
    rest = jnp.where(oh1, -jnp.inf, logits)
    m2 = jnp.max(rest, axis=1, keepdims=True)
    i2 = jnp.min(jnp.where(rest == m2, lane_f, float(LANES)), axis=1, keepdims=True)
    oh2 = lane_f == i2
    e21 = jnp.exp(m2 - m1)
    w1 = 1.0 / (1.0 + e21)
    w2 = e21 / (1.0 + e21)

    cnt = jnp.where(oh1, 1.0, jnp.where(oh2, 1.0, 0.0))
    before = (lax.broadcasted_iota(jnp.int32, (tm, tm), 1) < lax.broadcasted_iota(jnp.int32, (tm, tm), 0)).astype(BF16)
    rank = _dot(before, cnt.astype(BF16)) + carry_ref[0:1, :]
    r1 = jnp.sum(jnp.where(oh1, rank, 0.0), axis=1, keepdims=True)
    r2 = jnp.sum(jnp.where(oh2, rank, 0.0), axis=1, keepdims=True)
    total = carry_ref[0:1, :] + jnp.sum(cnt, axis=0, keepdims=True)
    carry_ref[...] = jnp.broadcast_to(total, carry_ref.shape)
    cnt_ref[...] = jnp.broadcast_to(total, cnt_ref.shape)

    packed = jnp.where(lane == 0, i1, jnp.where(lane == 1, i2, jnp.where(lane == 2, r1, jnp.where(lane == 3, r2, 0.0))))
    idx_ref[...] = packed.astype(jnp.int32)
    wt_ref[...] = jnp.where(lane == 0, w1, jnp.where(lane == 1, w2, 0.0))


def _route(logits, b_router_pad, tm=512):
    n = logits.shape[0]
    row = lambda i: (i, 0)
    return pl.pallas_call(
        functools.partial(_route_kernel, tm=tm),
        grid=(n // tm,),
        in_specs=[pl.BlockSpec((tm, LANES), row), pl.BlockSpec((1, LANES), lambda i: (0, 0))],
        out_specs=[pl.BlockSpec((tm, LANES), row), pl.BlockSpec((tm, LANES), row),
                   pl.BlockSpec((8, LANES), lambda i: (0, 0))],
        out_shape=[jax.ShapeDtypeStruct((n, LANES), jnp.int32), jax.ShapeDtypeStruct((n, LANES), F32),
                   jax.ShapeDtypeStruct((8, LANES), F32)],
        scratch_shapes=[pltpu.VMEM((8, LANES), F32)],
        compiler_params=_cparams("arbitrary"),
    )(logits, b_router_pad)


def _row_copy(src_hbm, src_row, dst, dst_row, sem):
    return pltpu.make_async_copy(src_hbm.at[pl.ds(src_row, 1)], dst.at[pl.ds(dst_row, 1)], sem)


def _gather_kernel(src_ref, used_ref, h_hbm, o_ref, buf, sem, *, tg):
    i = pl.program_id(0)
    base = i * tg

    @pl.when(used_ref[i] > 0)
    def _():
        def start(r, carry):
            _row_copy(h_hbm, src_ref[base + r], buf, r, sem).start()
            return carry

        lax.fori_loop(0, tg, start, 0, unroll=8)
        pltpu.make_async_copy(h_hbm.at[pl.ds(0, tg)], buf, sem).wait()
        o_ref[...] = buf[...].astype(o_ref.dtype)

    @pl.when(used_ref[i] == 0)
    def _():
        o_ref[...] = jnp.zeros_like(o_ref)


def _gather_rows(h, src, piece_used):
    tg = MOE_ROW_STEP
    m_pad = src.shape[0]
    d = h.shape[1]
    return pl.pallas_call(
        functools.partial(_gather_kernel, tg=tg),
        grid_spec=pltpu.PrefetchScalarGridSpec(
            num_scalar_prefetch=2,
            grid=(m_pad // tg,),
            in_specs=[pl.BlockSpec(memory_space=pl.ANY)],
            out_specs=pl.BlockSpec((tg, d), lambda i, src, used: (i, 0)),
            scratch_shapes=[pltpu.VMEM((tg, d), h.dtype), pltpu.SemaphoreType.DMA(())]),
        out_shape=jax.ShapeDtypeStruct((m_pad, d), BF16),
        compiler_params=_cparams("arbitrary"),
    )(src, piece_used, h)


def _group_swiglu_kernel(te_ref, tv_ref, a_ref, wg_ref, wu_ref, o_ref, wgb_ref, wub_ref):
    i = pl.program_id(1)
    new_expert = jnp.logical_or(i == 0, te_ref[i] != te_ref[jnp.maximum(i - 1, 0)])

    @pl.when(new_expert)
    def _():
        wgb_ref[...] = wg_ref[...].astype(BF16)
        wub_ref[...] = wu_ref[...].astype(BF16)

    tm = a_ref.shape[0]
    n_steps = tv_ref[i]

    def compute(n_rows):
        a = a_ref[:n_rows, :]
        g = _dot(a, wgb_ref[...])
        u = _dot(a, wub_ref[...])
        o_ref[:n_rows, :] = (_silu(g) * u).astype(o_ref.dtype)
        if n_rows < tm:
            o_ref[n_rows:, :] = jnp.zeros((tm - n_rows, o_ref.shape[1]), o_ref.dtype)

    for s in range(1, tm // MOE_ROW_STEP + 1):
        pl.when(n_steps == s)(functools.partial(compute, s * MOE_ROW_STEP))

    @pl.when(n_steps == 0)
    def _():
        o_ref[...] = jnp.zeros_like(o_ref)


def _group_swiglu(xs, w_gu, layer, tile_expert, tile_valid, tm, tn=512):
    m_pad, k = xs.shape
    f = w_gu.shape[-1] // 2
    nt = f // tn
    return pl.pallas_call(
        _group_swiglu_kernel,
        grid_spec=pltpu.PrefetchScalarGridSpec(
            num_scalar_prefetch=2,
            grid=(nt, m_pad // tm),
            in_specs=[pl.BlockSpec((tm, k), lambda n, i, te, tv: (jnp.where(tv[i] > 0, i, 0), 0)),
                      pl.BlockSpec((None, None, k, tn), lambda n, i, te, tv: (layer, te[i], 0, n)),
                      pl.BlockSpec((None, None, k, tn), lambda n, i, te, tv: (layer, te[i], 0, nt + n))],
            out_specs=pl.BlockSpec((tm, tn), lambda n, i, te, tv: (i, n)),
            scratch_shapes=[pltpu.VMEM((k, tn), BF16), pltpu.VMEM((k, tn), BF16)]),
        out_shape=jax.ShapeDtypeStruct((m_pad, f), BF16),
        compiler_params=_cparams("arbitrary", "arbitrary"),
    )(tile_expert, tile_valid, xs, w_gu, w_gu)


def _group_down_kernel(te_ref, tv_ref, a_ref, w_ref, o_ref, *, nk):
    i = pl.program_id(0)
    kk = pl.program_id(1)

    @pl.when(kk == 0)
    def _():
        o_ref[...] = jnp.zeros_like(o_ref)

    n_steps = tv_ref[i]

    def accumulate(n_rows):
        o_ref[:n_rows, :] += _dot(a_ref[:n_rows, :], w_ref[...].astype(BF16))

    for s in range(1, a_ref.shape[0] // MOE_ROW_STEP + 1):
        pl.when(n_steps == s)(functools.partial(accumulate, s * MOE_ROW_STEP))


def _group_down(hm, w_down, tile_expert, tile_valid, tm, tk=1024):
    m_pad, f = hm.shape
    d = w_down.shape[-1]
    nk = f // tk
    return pl.pallas_call(
        functools.partial(_group_down_kernel, nk=nk),
        grid_spec=pltpu.PrefetchScalarGridSpec(
            num_scalar_prefetch=2,
            grid=(m_pad // tm, nk),
            in_specs=[pl.BlockSpec((tm, tk), lambda i, kk, te, tv: (i, jnp.where(tv[i] > 0, kk, 0))),
                      pl.BlockSpec((None, tk, d), lambda i, kk, te, tv: (te[i], jnp.where(tv[i] > 0, kk, 0), 0))],
            out_specs=pl.BlockSpec((tm, d), lambda i, kk, te, tv: (i, 0))),
        out_shape=jax.ShapeDtypeStruct((m_pad, d), F32),
        compiler_params=_cparams("arbitrary", "arbitrary"),
    )(tile_expert, tile_valid, hm, w_down)


def _combine_ln_kernel(d1_ref, d2_ref, o_hbm, wt_ref, x_ref, gate_ref, g_ref, b_ref, xo_ref, buf1, buf2, sem, *, tm):
    base = pl.program_id(0) * tm

    def start(r, carry):
        _row_copy(o_hbm, d1_ref[base + r], buf1, r, sem).start()
        _row_copy(o_hbm, d2_ref[base + r], buf2, r, sem).start()
        return carry

    lax.fori_loop(0, tm, start, 0, unroll=8)
    pltpu.make_async_copy(o_hbm.at[pl.ds(0, tm)], buf1, sem).wait()
    pltpu.make_async_copy(o_hbm.at[pl.ds(0, tm)], buf2, sem).wait()
    wt = wt_ref[...]
    y = wt[:, 0:1] * buf1[...] + wt[:, 1:2] * buf2[...]
    r = DN_ALPHA * x_ref[...] + gate_ref[0] * y
    xo_ref[...] = _layer_norm(r, g_ref[...], b_ref[...])


def _combine_ln(o_sorted, d1, d2, wts, xf, gate, ln_g, ln_b, seq, tm=256):
    n, d = xf.shape
    row = lambda i, d1, d2: (i, 0)
    const = lambda i, d1, d2: (0, 0)
    return pl.pallas_call(
        functools.partial(_combine_ln_kernel, tm=tm),
        grid_spec=pltpu.PrefetchScalarGridSpec(
            num_scalar_prefetch=2,
            grid=(n // tm,),
            in_specs=[pl.BlockSpec(memory_space=pl.ANY),
                      pl.BlockSpec((tm, LANES), row),
                      pl.BlockSpec((tm, d), row),
                      pl.BlockSpec((1, 1, d), lambda i, d1, d2: (i * tm // seq, 0, 0)),
                      pl.BlockSpec((1, d), const),
                      pl.BlockSpec((1, d), const)],
            out_specs=pl.BlockSpec((tm, d), row),
            scratch_shapes=[pltpu.VMEM((tm, d), F32), pltpu.VMEM((tm, d), F32), pltpu.SemaphoreType.DMA(())]),
        out_shape=jax.ShapeDtypeStruct((n, d), F32),
        compiler_params=_cparams("arbitrary"),
    )(d1, d2, o_sorted, wts, xf, gate, ln_g.reshape(1, d), ln_b.reshape(1, d))


def kernel(x, c, w_ada, b_ada, ln_g, ln_b, gdn_w_in, gdn_conv_w, gdn_a_log, gdn_dt_bias, gdn_norm_w, gdn_w_out,
           kv_w_ada, kv_b_ada, sb_w_kv, sb_w_q, sb_w_out, ffn_w_gu, ffn_w_down, moe_w_router, moe_b_router,
           moe_w_gu, moe_w_down):
    bsz, seq, d = x.shape
    n = bsz * seq
    assert w_ada.shape[0] == DEPTH and gdn_w_in.shape[0] == 1 and sb_w_q.shape[0] == 1
    n_vh = gdn_a_log.shape[1]
    n_qk = n_vh // 2
    key_dim = n_qk * HEAD_DIM
    value_dim = n_vh * HEAD_DIM
    main_cols = 2 * key_dim + 2 * value_dim
    n_sb_heads = sb_w_q.shape[-1] // HEAD_DIM
    xf = x.reshape(n, d)

    c_pad = jnp.zeros((8, d), F32).at[:bsz].set(c)
    def mod_vectors(w, layer, b, parts):
        m = _ada(c_pad, w, layer, b)[:bsz]
        return [v.reshape(bsz, 1, d) for v in jnp.split(m, parts, axis=-1)]
    sh_m0, sc_m0, gt_m0, sh_f0, sc_f0, gt_f0 = mod_vectors(w_ada, 0, b_ada[0], 6)
    sh_m1, sc_m1, gt_m1, sh_f1, sc_f1, gt_f1 = mod_vectors(w_ada, 1, b_ada[1], 6)
    sh_kv, sc_kv = mod_vectors(kv_w_ada[None], 0, kv_b_ada, 2)

    h = _modulate(xf, sh_m0, sc_m0, seq)
    w_in_t = jnp.swapaxes(gdn_w_in, 1, 2)
    proj = _ws_mm(h, w_in_t, 0, main_cols, F32, transposed=True)
    gate_logits = _narrow_mm(h, w_in_t, 0, main_cols, 2 * n_vh)
    prm = jnp.zeros((8, LANES), F32)
    prm = prm.at[0, n_vh:2 * n_vh].set(gdn_dt_bias[0].astype(F32))
    prm = prm.at[1, n_vh:2 * n_vh].set(-jnp.exp(gdn_a_log[0].astype(F32)))
    conv_wt = jnp.transpose(gdn_conv_w[0])
    lg3 = gate_logits.reshape(bsz, seq, LANES)
    lgt = jnp.transpose(lg3[:, :, n_vh:2 * n_vh], (0, 2, 1)).reshape(bsz, n_vh, seq // GDN_CHUNK, GDN_CHUNK)
    prow = jnp.broadcast_to(prm[0:2, n_vh:2 * n_vh].T[:, :, None], (n_vh, 2, GDN_CHUNK))
    o = _gdn(proj.reshape(bsz, seq, main_cols), lg3, lgt, prm, prow, conv_wt,
             gdn_norm_w[0].reshape(1, HEAD_DIM), n_qk)
    x1, h1 = _proj_ln(o.reshape(n, value_dim), gdn_w_out[0].astype(BF16), xf, gt_m0, ln_g[0, 0], ln_b[0, 0],
                      [(sh_f0, sc_f0)], seq)

    hm = _ws_swiglu(h1, ffn_w_gu, 0)
    x2, h2, h_kv = _proj_ln(hm, ffn_w_down[0].astype(BF16), x1, gt_f0, ln_g[0, 1], ln_b[0, 1],
                            [(sh_m1, sc_m1), (sh_kv, sc_kv)], seq)

    kv = _ws_mm(h_kv, sb_w_kv[None], 0, 2 * n_sb_heads * HEAD_DIM, BF16)
    q = _ws_mm(h2, sb_w_q, 0, n_sb_heads * HEAD_DIM, BF16)
    oa = _sb_attention(q.reshape(bsz, seq, -1), kv.reshape(bsz, seq, -1), n_sb_heads)
    w_router = jnp.zeros((d, LANES), F32).at[:, :N_EXPERTS].set(moe_w_router[0])
    x3, h3, logits = _proj_ln(oa.reshape(n, -1), sb_w_out[0].astype(BF16), x2, gt_m1, ln_g[1, 0], ln_b[1, 0],
                              [(sh_f1, sc_f1)], seq, h_dtype=F32, w_router=w_router)

    b_router = jnp.zeros((1, LANES), F32).at[0, :N_EXPERTS].set(moe_b_router[0].astype(F32))
    idx, wts, cnt = _route(logits, b_router)
    tm_g = MOE_TILE
    n_tiles = 2 * n // tm_g + N_EXPERTS
    counts = cnt[0, :N_EXPERTS].astype(jnp.int32)
    tiles_per = (counts + tm_g - 1) // tm_g
    tile_end = jnp.cumsum(tiles_per)
    tile_first = tile_end - tiles_per
    row_off = tile_first * tm_g
    tile_ids = jnp.arange(n_tiles, dtype=jnp.int32)
    tile_expert = jnp.minimum(jnp.searchsorted(tile_end, tile_ids, side="right"), N_EXPERTS - 1).astype(jnp.int32)
    rows_left = counts[tile_expert] - (tile_ids - tile_first[tile_expert]) * tm_g
    tile_rows = jnp.where(tile_ids < tile_end[-1], jnp.clip(rows_left, 0, tm_g), 0)
    tile_steps = ((tile_rows + MOE_ROW_STEP - 1) // MOE_ROW_STEP).astype(jnp.int32)
    pieces = tm_g // MOE_ROW_STEP
    piece_used = (jnp.arange(pieces, dtype=jnp.int32)[None, :] < tile_steps[:, None]).astype(jnp.int32).reshape(-1)
    dest1 = row_off[idx[:, 0]] + idx[:, 2]
    dest2 = row_off[idx[:, 1]] + idx[:, 3]
    tok = jnp.arange(n, dtype=jnp.int32)
    src = jnp.zeros((n_tiles * tm_g,), jnp.int32).at[dest1].set(tok).at[dest2].set(tok)

    xs = _gather_rows(h3, src, piece_used)
    hm2 = _group_swiglu(xs, moe_w_gu, 0, tile_expert, tile_steps, tm_g)
    o_sorted = _group_down(hm2, moe_w_down[0], tile_expert, tile_steps, tm_g)
    out = _combine_ln(o_sorted, dest1, dest2, wts, x3, gt_f1, ln_g[1, 1], ln_b[1, 1], seq)
    return out.reshape(bsz, seq, d)
```

```python
import functools

import jax
import jax.numpy as jnp
from jax import lax
from jax.experimental import pallas as pl
from jax.experimental.pallas import tpu as pltpu

F32 = jnp.float32
BF16 = jnp.bfloat16

LANES = 128
VMEM_LIMIT_BYTES = 56 * 1024 * 1024

HEAD_DIM = 128
GDN_CHUNK = 64
CONV_WIDTH = 4
N_EXPERTS = 8
MOE_TILE = 1024
MOE_ROW_STEP = 256
LN_EPS = 1e-5
RMS_EPS = 1e-6
L2_EPS = 1e-6
DEPTH = 2
DN_ALPHA = (2.0 * DEPTH) ** 0.25


def _cparams(*sem):
    return pltpu.CompilerParams(dimension_semantics=sem, vmem_limit_bytes=VMEM_LIMIT_BYTES)


def _dot(a, b):
    return jnp.dot(a, b, preferred_element_type=F32)


def _dot_nt(a, b):
    return lax.dot_general(a, b, (((1,), (1,)), ((), ())), preferred_element_type=F32)


def _dot_tn(a, b):
    return lax.dot_general(a, b, (((0,), (0,)), ((), ())), preferred_element_type=F32)


def _split(a):
    hi = a.astype(BF16)
    lo = (a - hi.astype(F32)).astype(BF16)
    return hi, lo


def _dot3(a, b):
    ah, al = _split(a)
    bh, bl = _split(b)
    return _dot(ah, bh) + (_dot(ah, bl) + _dot(al, bh))


def _softplus(x):
    return jnp.maximum(x, 0.0) + jnp.log(1.0 + jnp.exp(-jnp.abs(x)))


def _silu(x):
    return x * jax.nn.sigmoid(x)


def _ada_kernel(c_ref, w_ref, b_ref, o_ref):
    s = _silu(c_ref[...]).astype(BF16)
    o_ref[...] = _dot(s, w_ref[...].astype(BF16)) + b_ref[...]


def _ada(c_pad, w, layer, b, tn=1024):
    _, d, n_out = w.shape
    return pl.pallas_call(
        _ada_kernel,
        grid=(n_out // tn,),
        in_specs=[pl.BlockSpec((8, d), lambda n: (0, 0)),
                  pl.BlockSpec((None, d, tn), lambda n: (layer, 0, n)),
                  pl.BlockSpec((1, tn), lambda n: (0, n))],
        out_specs=pl.BlockSpec((8, tn), lambda n: (0, n)),
        out_shape=jax.ShapeDtypeStruct((8, n_out), F32),
        compiler_params=_cparams("arbitrary"),
    )(c_pad, w, b.reshape(1, n_out))


def _modulate_kernel(x_ref, sh_ref, sc_ref, o_ref):
    o_ref[...] = (x_ref[...] * (1.0 + sc_ref[0]) + sh_ref[0]).astype(o_ref.dtype)


def _modulate(xf, shift, scale, seq, tm=512):
    n, d = xf.shape
    bmap = lambda i: (i * tm // seq, 0, 0)
    return pl.pallas_call(
        _modulate_kernel,
        grid=(n // tm,),
        in_specs=[pl.BlockSpec((tm, d), lambda i: (i, 0)),
                  pl.BlockSpec((1, 1, d), bmap),
                  pl.BlockSpec((1, 1, d), bmap)],
        out_specs=pl.BlockSpec((tm, d), lambda i: (i, 0)),
        out_shape=jax.ShapeDtypeStruct((n, d), BF16),
        compiler_params=_cparams("arbitrary"),
    )(xf, shift, scale)


def _ws_mm_kernel(a_ref, w_ref, o_ref, wb_ref, *, transposed):
    @pl.when(pl.program_id(1) == 0)
    def _():
        w = w_ref[...]
        wb_ref[...] = (w.T if transposed else w).astype(BF16)

    o_ref[...] = _dot(a_ref[...], wb_ref[...]).astype(o_ref.dtype)


def _ws_mm(a, w, layer, n_out, out_dtype, transposed=False, tm=1024, tn=1024):
    m, k = a.shape
    tn = min(tn, n_out)
    if transposed:
        w_spec = pl.BlockSpec((None, tn, k), lambda n, i: (layer, n, 0))
    else:
        w_spec = pl.BlockSpec((None, k, tn), lambda n, i: (layer, 0, n))
    return pl.pallas_call(
        functools.partial(_ws_mm_kernel, transposed=transposed),
        grid=(n_out // tn, m // tm),
        in_specs=[pl.BlockSpec((tm, k), lambda n, i: (i, 0)), w_spec],
        out_specs=pl.BlockSpec((tm, tn), lambda n, i: (i, n)),
        out_shape=jax.ShapeDtypeStruct((m, n_out), out_dtype),
        scratch_shapes=[pltpu.VMEM((k, tn), BF16)],
        compiler_params=_cparams("arbitrary", "arbitrary"),
    )(a, w)


def _narrow_mm_kernel(a_ref, w_hbm, o_ref, wf_ref, wb_ref, sem, *, layer, row0, n_rows):
    @pl.when(pl.program_id(0) == 0)
    def _():
        copy = pltpu.make_async_copy(w_hbm.at[layer, pl.ds(row0, n_rows), :], wf_ref, sem)
        copy.start()
        copy.wait()
        wb_ref[...] = jnp.zeros_like(wb_ref)
        wb_ref[:n_rows, :] = wf_ref[...].astype(BF16)

    o_ref[...] = _dot_nt(a_ref[...], wb_ref[...])


def _narrow_mm(a, w_t, layer, row0, n_rows, tm=1024):
    m, k = a.shape
    return pl.pallas_call(
        functools.partial(_narrow_mm_kernel, layer=layer, row0=row0, n_rows=n_rows),
        grid=(m // tm,),
        in_specs=[pl.BlockSpec((tm, k), lambda i: (i, 0)),
                  pl.BlockSpec(memory_space=pl.ANY)],
        out_specs=pl.BlockSpec((tm, LANES), lambda i: (i, 0)),
        out_shape=jax.ShapeDtypeStruct((m, LANES), F32),
        scratch_shapes=[pltpu.VMEM((n_rows, k), F32), pltpu.VMEM((LANES, k), BF16), pltpu.SemaphoreType.DMA(())],
        compiler_params=_cparams("arbitrary"),
    )(a, w_t)


def _ws_swiglu_kernel(a_ref, wg_ref, wu_ref, o_ref, wgb_ref, wub_ref):
    @pl.when(pl.program_id(1) == 0)
    def _():
        wgb_ref[...] = wg_ref[...].astype(BF16)
        wub_ref[...] = wu_ref[...].astype(BF16)

    a = a_ref[...]
    g = _dot(a, wgb_ref[...])
    u = _dot(a, wub_ref[...])
    o_ref[...] = (_silu(g) * u).astype(o_ref.dtype)


def _ws_swiglu(a, w_gu, layer, tm=1024, tn=512):
    m, k = a.shape
    f = w_gu.shape[-1] // 2
    nt = f // tn
    return pl.pallas_call(
        _ws_swiglu_kernel,
        grid=(nt, m // tm),
        in_specs=[pl.BlockSpec((tm, k), lambda n, i: (i, 0)),
                  pl.BlockSpec((None, k, tn), lambda n, i: (layer, 0, n)),
                  pl.BlockSpec((None, k, tn), lambda n, i: (layer, 0, nt + n))],
        out_specs=pl.BlockSpec((tm, tn), lambda n, i: (i, n)),
        out_shape=jax.ShapeDtypeStruct((m, f), BF16),
        scratch_shapes=[pltpu.VMEM((k, tn), BF16), pltpu.VMEM((k, tn), BF16)],
        compiler_params=_cparams("arbitrary", "arbitrary"),
    )(a, w_gu, w_gu)


def _layer_norm(r, g, b):
    mu = jnp.mean(r, axis=-1, keepdims=True)
    rc = r - mu
    var = jnp.mean(rc * rc, axis=-1, keepdims=True)
    return rc * lax.rsqrt(var + LN_EPS) * g + b


def _proj_ln_kernel(*refs, n_mod, nk, with_router):
    a_ref, w_ref, x_ref, gate_ref, g_ref, b_ref = refs[:6]
    mod_refs = refs[6:6 + 2 * n_mod]
    pos = 6 + 2 * n_mod
    wr_ref = refs[pos] if with_router else None
    pos += int(with_router)
    xo_ref = refs[pos]
    h_refs = refs[pos + 1:pos + 1 + n_mod]
    pos += 1 + n_mod
    lg_ref = refs[pos] if with_router else None
    acc_ref = refs[-1]
    kk = pl.program_id(1)

    @pl.when(kk == 0)
    def _():
        acc_ref[...] = jnp.zeros_like(acc_ref)

    acc_ref[...] += _dot(a_ref[...], w_ref[...])

    @pl.when(kk == nk - 1)
    def _():
        r = DN_ALPHA * x_ref[...] + gate_ref[0] * acc_ref[...]
        xn = _layer_norm(r, g_ref[...], b_ref[...])
        xo_ref[...] = xn
        for t in range(n_mod):
            h = xn * (1.0 + mod_refs[2 * t + 1][0]) + mod_refs[2 * t][0]
            h_refs[t][...] = h.astype(h_refs[t].dtype)
            if with_router and t == 0:
                lg_ref[...] = _dot3(h, wr_ref[...])


def _k_tile(k, limit=2048, unit=256):
    nk = 1
    while k % nk or k // nk > limit or (k // nk) % unit:
        nk += 1
    return k // nk


def _proj_ln(a, w, xf, gate, ln_g, ln_b, mods, seq, h_dtype=BF16, w_router=None, tm=512):
    m, k = a.shape
    d = w.shape[-1]
    tk = _k_tile(k)
    nk = k // tk
    n_mod = len(mods)
    with_router = w_router is not None
    bmap = lambda i, kk: (i * tm // seq, 0, 0)
    row = lambda i, kk: (i, 0)
    const = lambda i, kk: (0, 0)
    in_specs = [pl.BlockSpec((tm, tk), lambda i, kk: (i, kk)),
                pl.BlockSpec((tk, d), lambda i, kk: (kk, 0)),
                pl.BlockSpec((tm, d), row),
                pl.BlockSpec((1, 1, d), bmap),
                pl.BlockSpec((1, d), const),
                pl.BlockSpec((1, d), const)]
    args = [a, w, xf, gate, ln_g.reshape(1, d), ln_b.reshape(1, d)]
    for sh, sc in mods:
        in_specs += [pl.BlockSpec((1, 1, d), bmap), pl.BlockSpec((1, 1, d), bmap)]
        args += [sh, sc]
    out_specs = [pl.BlockSpec((tm, d), row)] + [pl.BlockSpec((tm, d), row)] * n_mod
    out_shape = [jax.ShapeDtypeStruct((m, d), F32)] + [jax.ShapeDtypeStruct((m, d), h_dtype)] * n_mod
    if with_router:
        in_specs.append(pl.BlockSpec((d, LANES), const))
        args.append(w_router)
        out_specs.append(pl.BlockSpec((tm, LANES), row))
        out_shape.append(jax.ShapeDtypeStruct((m, LANES), F32))
    return pl.pallas_call(
        functools.partial(_proj_ln_kernel, n_mod=n_mod, nk=nk, with_router=with_router),
        grid=(m // tm, nk),
        in_specs=in_specs,
        out_specs=out_specs,
        out_shape=out_shape,
        scratch_shapes=[pltpu.VMEM((tm, d), F32)],
        compiler_params=_cparams("arbitrary", "arbitrary"),
    )(*args)


GDN_GROUP = 8


def _gdn_kernel(q_ref, k_ref, v_ref, z_ref, cwq_ref, cwk_ref, cwv_ref, lg_ref, lgt_ref, prm_ref, prow_ref, nw_ref,
                o_ref, gcr_s, kw_r, nn_r, qp_r, op_r, dg_r):
    j = pl.program_id(1)
    seq = q_ref.shape[1]
    c = GDN_CHUNK
    hd = HEAD_DIM
    n_groups = seq // (c * GDN_GROUP)
    n_vh = 2 * pl.num_programs(1)
    halo = 8

    def window(ref, ch):
        if isinstance(ch, int) and ch == 0:
            body = ref[0, 0:c, :]
            return jnp.concatenate([jnp.zeros((halo, body.shape[1]), F32), body], axis=0)
        if isinstance(ch, int):
            return ref[0, ch * c - halo:(ch + 1) * c, :]
        return ref[0, pl.ds(pl.multiple_of(ch * c - halo, halo), c + halo), :]

    def conv_silu(xw, cw):
        acc = xw[halo:, :] * cw[CONV_WIDTH - 1:CONV_WIDTH, :]
        for s in range(1, CONV_WIDTH):
            acc = acc + xw[halo - s:halo - s + c, :] * cw[CONV_WIDTH - 1 - s:CONV_WIDTH - s, :]
        return _silu(acc)

    def l2norm(x):
        return x * lax.rsqrt(jnp.sum(x * x, axis=-1, keepdims=True) + L2_EPS)

    ri = lax.broadcasted_iota(jnp.int32, (c, c), 0)
    ci = lax.broadcasted_iota(jnp.int32, (c, c), 1)
    upper = (ri <= ci).astype(BF16)
    for hh in range(2):
        g_row = prow_ref[hh, 1:2, :] * _softplus(lgt_ref[0, hh] + prow_ref[hh, 0:1, :])
        g1, g2 = _split(g_row)
        g3 = (g_row - g1.astype(F32) - g2.astype(F32)).astype(BF16)
        gcr_s[hh] = _dot(g1, upper) + (_dot(g2, upper) + _dot(g3, upper))

    incl = ri >= ci
    strict = ri > ci
    lane = lax.broadcasted_iota(jnp.int32, (c, LANES), 1)
    pos = lax.broadcasted_iota(jnp.int32, (c, 1), 0)

    def select_lane(x, idx):
        return jnp.sum(jnp.where(lane == idx, x, 0.0), axis=1, keepdims=True)

    def prepare_group(first_chunk, slot):
        chunks = []
        for k in range(GDN_GROUP):
            ch = first_chunk + k
            rows = pl.ds(ch * c, c) if isinstance(ch, int) else pl.ds(pl.multiple_of(ch * c, c), c)
            qc = l2norm(conv_silu(window(q_ref, ch), cwq_ref[...])) * (hd ** -0.5)
            kc = l2norm(conv_silu(window(k_ref, ch), cwk_ref[...]))
            vc = conv_silu(window(v_ref, ch), cwv_ref[...])
            chunks.append((ch, rows, kc, qc, vc, kc.astype(BF16), qc.astype(BF16)))
        kks = [_dot_nt(kcb, kcb) for (_, _, _, _, _, kcb, _) in chunks]
        qks = [_dot_nt(qcb, kcb) for (_, _, _, _, _, kcb, qcb) in chunks]
        probs = []
        for (ch, rows, kc, qc, vc, _, _), kk, qk in zip(chunks, kks, qks):
            lgc = lg_ref[0, rows, :]
            gac = prm_ref[1:2, :] * _softplus(lgc + prm_ref[0:1, :])
            for hh in range(2):
                head = 2 * j + hh
                beta = jax.nn.sigmoid(select_lane(lgc, head))
                gc = jnp.broadcast_to(select_lane(gac, n_vh + head), (c, hd))
                s = 1
                while s < c:
                    gc = gc + jnp.where(pos >= s, pltpu.roll(gc, s, axis=0), 0.0)
                    s *= 2
                gl = jnp.broadcast_to(gc[c - 1:c, :], (c, hd))
                eg = jnp.exp(gc)
                row_b = jnp.broadcast_to(gcr_s[hh, pl.ds(ch, 1), :], (c, c))
                dec = jnp.exp(jnp.where(incl, gc[:, :c] - row_b, -jnp.inf))
                low = jnp.where(strict, kk * beta * dec, 0.0)
                rhs = jnp.concatenate([vc[:, hh * hd:(hh + 1) * hd] * beta, kc * beta * eg], axis=1)
                probs.append(dict(low=low, rhs=rhs, at=(qk * dec).astype(BF16), qd=qc * eg,
                                  kd=(kc * jnp.exp(gl - gc)).astype(BF16), dg=jnp.exp(gl[:8, :])))
        yield
        eye = (ri == ci).astype(F32)
        ps = [p["low"] for p in probs]
        xs = [eye - p["low"] for p in probs]
        n_terms = 2
        while n_terms < c:
            pbs = [p.astype(BF16) for p in ps]
            ps = [_dot(pb, pb) for pb in pbs]
            xs = [x + _dot(x.astype(BF16), p.astype(BF16)) for x, p in zip(xs, ps)]
            n_terms *= 2
            if n_terms in (4, 16, 64):
                yield
        sols = [_dot(x.astype(BF16), p["rhs"].astype(BF16)) for x, p in zip(xs, probs)]
        ubs = [sol[:, :hd].astype(BF16) for sol in sols]
        wbs = [sol[:, hd:].astype(BF16) for sol in sols]
        kws = [_dot_tn(p["kd"], wb) for p, wb in zip(probs, wbs)]
        nns = [_dot_tn(p["kd"], ub) for p, ub in zip(probs, ubs)]
        aws = [_dot(p["at"], wb) for p, wb in zip(probs, wbs)]
        ops = [_dot(p["at"], ub) for p, ub in zip(probs, ubs)]
        for n, p in enumerate(probs):
            k, hh = divmod(n, 2)
            kw_r[slot, k, hh] = kws[n].astype(BF16)
            nn_r[slot, k, hh] = nns[n]
            qp_r[slot, k, hh] = (p["qd"] - aws[n]).astype(BF16)
            op_r[slot, k, hh] = ops[n]
            dg_r[slot, k, hh] = p["dg"]

    def scan(ch, slot, k, states):
        rows = pl.ds(pl.multiple_of(ch * c, c), c)
        new_states = []
        for hh in range(2):
            st = states[hh]
            sb = st.astype(BF16)
            o = _dot(qp_r[slot, k, hh], sb) + op_r[slot, k, hh]
            decay = jnp.broadcast_to(dg_r[slot, k, hh][0:1, :], (hd, hd))
            new_states.append(st * decay - _dot(kw_r[slot, k, hh], sb) + nn_r[slot, k, hh])
            o = o * lax.rsqrt(jnp.mean(o * o, axis=-1, keepdims=True) + RMS_EPS) * nw_ref[...]
            cols = slice(hh * hd, (hh + 1) * hd)
            o_ref[0, rows, cols] = (o * _silu(z_ref[0, rows, cols])).astype(o_ref.dtype)
        return tuple(new_states)

    for _ in prepare_group(0, 0):
        pass

    def body(g, states):
        slot = g % 2
        stages = prepare_group((g + 1) * GDN_GROUP, 1 - slot)
        for k in range(GDN_GROUP):
            if k % (GDN_GROUP // 4) == 0:
                next(stages)
            states = scan(g * GDN_GROUP + k, slot, k, states)
        for _ in stages:
            pass
        return states

    zero = jnp.zeros((hd, hd), F32)
    states = lax.fori_loop(0, n_groups - 1, body, (zero, zero))
    last = n_groups - 1
    for k in range(GDN_GROUP):
        states = scan(last * GDN_GROUP + k, last % 2, k, states)


def _gdn(proj, lg, lgt, prm, prow, conv_wt, norm_w, n_qk_heads):
    bsz, seq, _ = proj.shape
    hq = n_qk_heads
    hd = HEAD_DIM
    n_chunks = seq // GDN_CHUNK
    ring = (2, GDN_GROUP, 2)
    return pl.pallas_call(
        _gdn_kernel,
        grid=(bsz, hq),
        in_specs=[pl.BlockSpec((1, seq, hd), lambda b, j: (b, 0, j)),
                  pl.BlockSpec((1, seq, hd), lambda b, j: (b, 0, hq + j)),
                  pl.BlockSpec((1, seq, 2 * hd), lambda b, j: (b, 0, hq + j)),
                  pl.BlockSpec((1, seq, 2 * hd), lambda b, j: (b, 0, 2 * hq + j)),
                  pl.BlockSpec((CONV_WIDTH, hd), lambda b, j: (0, j)),
                  pl.BlockSpec((CONV_WIDTH, hd), lambda b, j: (0, hq + j)),
                  pl.BlockSpec((CONV_WIDTH, 2 * hd), lambda b, j: (0, hq + j)),
                  pl.BlockSpec((1, seq, LANES), lambda b, j: (b, 0, 0)),
                  pl.BlockSpec((1, 2, n_chunks, GDN_CHUNK), lambda b, j: (b, j, 0, 0)),
                  pl.BlockSpec((8, LANES), lambda b, j: (0, 0)),
                  pl.BlockSpec((2, 2, GDN_CHUNK), lambda b, j: (j, 0, 0)),
                  pl.BlockSpec((1, hd), lambda b, j: (0, 0))],
        out_specs=pl.BlockSpec((1, seq, 2 * hd), lambda b, j: (b, 0, j)),
        out_shape=jax.ShapeDtypeStruct((bsz, seq, 2 * hq * hd), BF16),
        scratch_shapes=[pltpu.VMEM((2, n_chunks, GDN_CHUNK), F32),
                        pltpu.VMEM(ring + (hd, hd), BF16),
                        pltpu.VMEM(ring + (hd, hd), F32),
                        pltpu.VMEM(ring + (GDN_CHUNK, hd), BF16),
                        pltpu.VMEM(ring + (GDN_CHUNK, hd), F32),
                        pltpu.VMEM(ring + (8, hd), F32)],
        compiler_params=_cparams("arbitrary", "arbitrary"),
    )(proj, proj, proj, proj, conv_wt, conv_wt, conv_wt, lg, lgt, prm, prow, norm_w)


SB_HEADS_PER_STEP = 4
SB_ZERO_EXPONENT = -110.0


def _sb_kernel(q_ref, k_ref, v_ref, o_ref, kmax_s, *, blk):
    i = pl.program_id(2)
    hd = HEAD_DIM
    scale = hd ** -0.5
    heads = range(SB_HEADS_PER_STEP)

    @pl.when(i == 0)
    def _():
        for hh in heads:
            kf = k_ref[0, :, hh * hd:(hh + 1) * hd].astype(F32)
            n2 = jnp.max(jnp.sum(kf * kf, axis=1, keepdims=True), axis=0, keepdims=True)
            kmax_s[hh] = jnp.broadcast_to(jnp.sqrt(n2) * scale, kmax_s.shape[1:])

    ri = lax.broadcasted_iota(jnp.int32, (blk, blk), 0)
    ci = lax.broadcasted_iota(jnp.int32, (blk, blk), 1)
    tri = (ri >= ci).astype(BF16)
    causal = ci < ri
    qs = [q_ref[0, :, hh * hd:(hh + 1) * hd] for hh in heads]
    z_bound = []
    for hh in heads:
        qf = qs[hh].astype(F32)
        z_bound.append(jnp.sqrt(jnp.sum(qf * qf, axis=1, keepdims=True)) * kmax_s[hh][0:1, 0:1])

    def block(jb, accs, runs, diagonal):
        rows = pl.ds(pl.multiple_of(jb * blk, blk), blk)
        cols = [slice(hh * hd, (hh + 1) * hd) for hh in heads]
        zs = [_dot_nt(qs[hh], k_ref[0, rows, cols[hh]]) * scale for hh in heads]
        lss = [-_softplus(z) for z in zs]
        if diagonal:
            lss = [jnp.where(causal, ls, 0.0) for ls in lss]
        parts = [_split(ls) for ls in lss]
        rcs = [_dot(hi, tri) + _dot(lo, tri) for hi, lo in parts]
        ws = [jnp.exp(zs[hh] + rcs[hh] + runs[hh]) for hh in heads]
        if diagonal:
            ws = [jnp.where(causal, w, 0.0) for w in ws]
        new_accs = [accs[hh] + _dot(ws[hh].astype(BF16), v_ref[0, rows, cols[hh]]) for hh in heads]
        new_runs = [runs[hh] + rcs[hh][:, :1] for hh in heads]
        return tuple(new_accs), tuple(new_runs)

    def any_weight_left(runs):
        m = jnp.max(runs[0] + z_bound[0])
        for hh in heads[1:]:
            m = jnp.maximum(m, jnp.max(runs[hh] + z_bound[hh]))
        return (m > SB_ZERO_EXPONENT).astype(jnp.int32)

    zeros = tuple(jnp.zeros((blk, hd), F32) for _ in heads)
    zrun = tuple(jnp.zeros((blk, 1), F32) for _ in heads)
    accs, runs = block(i, zeros, zrun, True)

    def cond(carry):
        return jnp.logical_and(carry[0] >= 0, carry[1] > 0)

    def body(carry):
        jb, _, accs, runs = carry
        accs, runs = block(jb, accs, runs, False)
        return jb - 1, any_weight_left(runs), accs, runs

    _, _, accs, _ = lax.while_loop(cond, body, (i - 1, any_weight_left(runs), accs, runs))
    for hh in heads:
        o_ref[0, :, hh * hd:(hh + 1) * hd] = accs[hh].astype(o_ref.dtype)


def _sb_attention(q, kv, n_heads, blk=256):
    bsz, seq, _ = q.shape
    w = SB_HEADS_PER_STEP * HEAD_DIM
    n_hp = n_heads // SB_HEADS_PER_STEP
    return pl.pallas_call(
        functools.partial(_sb_kernel, blk=blk),
        grid=(bsz, n_hp, seq // blk),
        in_specs=[pl.BlockSpec((1, blk, w), lambda b, h, i: (b, i, h)),
                  pl.BlockSpec((1, seq, w), lambda b, h, i: (b, 0, h)),
                  pl.BlockSpec((1, seq, w), lambda b, h, i: (b, 0, n_hp + h))],
        out_specs=pl.BlockSpec((1, blk, w), lambda b, h, i: (b, i, h)),
        out_shape=jax.ShapeDtypeStruct((bsz, seq, n_heads * HEAD_DIM), BF16),
        scratch_shapes=[pltpu.VMEM((SB_HEADS_PER_STEP, 8, LANES), F32)],
        compiler_params=_cparams("arbitrary", "arbitrary", "arbitrary"),
    )(q, kv, kv)


def _route_kernel(lg_ref, br_ref, idx_ref, wt_ref, cnt_ref, carry_ref, *, tm):
    i = pl.program_id(0)

    @pl.when(i == 0)
    def _():
        carry_ref[...] = jnp.zeros_like(carry_ref)

    lane = lax.broadcasted_iota(jnp.int32, (tm, LANES), 1)
    lane_f = lane.astype(F32)
    logits = jnp.where(lane < N_EXPERTS, lg_ref[...] + br_ref[...], -jnp.inf)
    m1 = jnp.max(logits, axis=1, keepdims=True)
    i1 = jnp.min(jnp.where(logits == m1, lane_f, float(LANES)), axis=1, keepdims=True)
    oh1 = lane_f == i1
    rest = jnp.where(oh1, -jnp.inf, logits)
    m2 = jnp.max(rest, axis=1, keepdims=True)
    i2 = jnp.min(jnp.where(rest == m2, lane_f, float(LANES)), axis=1, keepdims=True)
    oh2 = lane_f == i2
    e21 = jnp.exp(m2 - m1)
    w1 = 1.0 / (1.0 + e21)
    w2 = e21 / (1.0 + e21)

    cnt = jnp.where(oh1, 1.0, jnp.where(oh2, 1.0, 0.0))
    before = (lax.broadcasted_iota(jnp.int32, (tm, tm), 1) < lax.broadcasted_iota(jnp.int32, (tm, tm), 0)).astype(BF16)
    rank = _dot(before, cnt.astype(BF16)) + carry_ref[0:1, :]
    r1 = jnp.sum(jnp.where(oh1, rank, 0.0), axis=1, keepdims=True)
    r2 = jnp.sum(jnp.where(oh2, rank, 0.0), axis=1, keepdims=True)
    total = carry_ref[0:1, :] + jnp.sum(cnt, axis=0, keepdims=True)
    carry_ref[...] = jnp.broadcast_to(total, carry_ref.shape)
    cnt_ref[...] = jnp.broadcast_to(total, cnt_ref.shape)

    packed = jnp.where(lane == 0, i1, jnp.where(lane == 1, i2, jnp.where(lane == 2, r1, jnp.where(lane == 3, r2, 0.0))))
    idx_ref[...] = packed.astype(jnp.int32)
    wt_ref[...] = jnp.where(lane == 0, w1, jnp.where(lane == 1, w2, 0.0))


def _route(logits, b_router_pad, tm=512):
    n = logits.shape[0]
    row = lambda i: (i, 0)
    return pl.pallas_call(
        functools.partial(_route_kernel, tm=tm),
        grid=(n // tm,),
        in_specs=[pl.BlockSpec((tm, LANES), row), pl.BlockSpec((1, LANES), lambda i: (0, 0))],
        out_specs=[pl.BlockSpec((tm, LANES), row), pl.BlockSpec((tm, LANES), row),
                   pl.BlockSpec((8, LANES), lambda i: (0, 0))],
        out_shape=[jax.ShapeDtypeStruct((n, LANES), jnp.int32), jax.ShapeDtypeStruct((n, LANES), F32),
                   jax.ShapeDtypeStruct((8, LANES), F32)],
        scratch_shapes=[pltpu.VMEM((8, LANES), F32)],
        compiler_params=_cparams("arbitrary"),
    )(logits, b_router_pad)


def _row_copy(src_hbm, src_row, dst, dst_row, sem):
    return pltpu.make_async_copy(src_hbm.at[pl.ds(src_row, 1)], dst.at[pl.ds(dst_row, 1)], sem)


def _gather_kernel(src_ref, used_ref, h_hbm, o_ref, buf, sem, *, tg):
    i = pl.program_id(0)
    base = i * tg

    @pl.when(used_ref[i] > 0)
    def _():
        def start(r, carry):
            _row_copy(h_hbm, src_ref[base + r], buf, r, sem).start()
            return carry

        lax.fori_loop(0, tg, start, 0, unroll=8)
        pltpu.make_async_copy(h_hbm.at[pl.ds(0, tg)], buf, sem).wait()
        o_ref[...] = buf[...].astype(o_ref.dtype)

    @pl.when(used_ref[i] == 0)
    def _():
        o_ref[...] = jnp.zeros_like(o_ref)


def _gather_rows(h, src, piece_used):
    tg = MOE_ROW_STEP
    m_pad = src.shape[0]
    d = h.shape[1]
    return pl.pallas_call(
        functools.partial(_gather_kernel, tg=tg),
        grid_spec=pltpu.PrefetchScalarGridSpec(
            num_scalar_prefetch=2,
            grid=(m_pad // tg,),
            in_specs=[pl.BlockSpec(memory_space=pl.ANY)],
            out_specs=pl.BlockSpec((tg, d), lambda i, src, used: (i, 0)),
            scratch_shapes=[pltpu.VMEM((tg, d), h.dtype), pltpu.SemaphoreType.DMA(())]),
        out_shape=jax.ShapeDtypeStruct((m_pad, d), BF16),
        compiler_params=_cparams("arbitrary"),
    )(src, piece_used, h)


def _group_swiglu_kernel(te_ref, tv_ref, a_ref, wg_ref, wu_ref, o_ref, wgb_ref, wub_ref):
    i = pl.program_id(1)
    new_expert = jnp.logical_or(i == 0, te_ref[i] != te_ref[jnp.maximum(i - 1, 0)])

    @pl.when(new_expert)
    def _():
        wgb_ref[...] = wg_ref[...].astype(BF16)
        wub_ref[...] = wu_ref[...].astype(BF16)

    tm = a_ref.shape[0]
    n_steps = tv_ref[i]

    def compute(n_rows):
        a = a_ref[:n_rows, :]
        g = _dot(a, wgb_ref[...])
        u = _dot(a, wub_ref[...])
        o_ref[:n_rows, :] = (_silu(g) * u).astype(o_ref.dtype)
        if n_rows < tm:
            o_ref[n_rows:, :] = jnp.zeros((tm - n_rows, o_ref.shape[1]), o_ref.dtype)

    for s in range(1, tm // MOE_ROW_STEP + 1):
        pl.when(n_steps == s)(functools.partial(compute, s * MOE_ROW_STEP))

    @pl.when(n_steps == 0)
    def _():
        o_ref[...] = jnp.zeros_like(o_ref)


def _group_swiglu(xs, w_gu, layer, tile_expert, tile_valid, tm, tn=512):
    m_pad, k = xs.shape
    f = w_gu.shape[-1] // 2
    nt = f // tn
    return pl.pallas_call(
        _group_swiglu_kernel,
        grid_spec=pltpu.PrefetchScalarGridSpec(
            num_scalar_prefetch=2,
            grid=(nt, m_pad // tm),
            in_specs=[pl.BlockSpec((tm, k), lambda n, i, te, tv: (jnp.where(tv[i] > 0, i, 0), 0)),
                      pl.BlockSpec((None, None, k, tn), lambda n, i, te, tv: (layer, te[i], 0, n)),
                      pl.BlockSpec((None, None, k, tn), lambda n, i, te, tv: (layer, te[i], 0, nt + n))],
            out_specs=pl.BlockSpec((tm, tn), lambda n, i, te, tv: (i, n)),
            scratch_shapes=[pltpu.VMEM((k, tn), BF16), pltpu.VMEM((k, tn), BF16)]),
        out_shape=jax.ShapeDtypeStruct((m_pad, f), BF16),
        compiler_params=_cparams("arbitrary", "arbitrary"),
    )(tile_expert, tile_valid, xs, w_gu, w_gu)


def _group_down_kernel(te_ref, tv_ref, a_ref, w_ref, o_ref, *, nk):
    i = pl.program_id(0)
    kk = pl.program_id(1)

    @pl.when(kk == 0)
    def _():
        o_ref[...] = jnp.zeros_like(o_ref)

    n_steps = tv_ref[i]

    def accumulate(n_rows):
        o_ref[:n_rows, :] += _dot(a_ref[:n_rows, :], w_ref[...].astype(BF16))

    for s in range(1, a_ref.shape[0] // MOE_ROW_STEP + 1):
        pl.when(n_steps == s)(functools.partial(accumulate, s * MOE_ROW_STEP))


def _group_down(hm, w_down, tile_expert, tile_valid, tm, tk=1024):
    m_pad, f = hm.shape
    d = w_down.shape[-1]
    nk = f // tk
    return pl.pallas_call(
        functools.partial(_group_down_kernel, nk=nk),
        grid_spec=pltpu.PrefetchScalarGridSpec(
            num_scalar_prefetch=2,
            grid=(m_pad // tm, nk),
            in_specs=[pl.BlockSpec((tm, tk), lambda i, kk, te, tv: (i, jnp.where(tv[i] > 0, kk, 0))),
                      pl.BlockSpec((None, tk, d), lambda i, kk, te, tv: (te[i], jnp.where(tv[i] > 0, kk, 0), 0))],
            out_specs=pl.BlockSpec((tm, d), lambda i, kk, te, tv: (i, 0))),
        out_shape=jax.ShapeDtypeStruct((m_pad, d), F32),
        compiler_params=_cparams("arbitrary", "arbitrary"),
    )(tile_expert, tile_valid, hm, w_down)


def _combine_ln_kernel(d1_ref, d2_ref, o_hbm, wt_ref, x_ref, gate_ref, g_ref, b_ref, xo_ref, buf1, buf2, sem, *, tm):
    base = pl.program_id(0) * tm

    def start(r, carry):
        _row_copy(o_hbm, d1_ref[base + r], buf1, r, sem).start()
        _row_copy(o_hbm, d2_ref[base + r], buf2, r, sem).start()
        return carry

    lax.fori_loop(0, tm, start, 0, unroll=8)
    pltpu.make_async_copy(o_hbm.at[pl.ds(0, tm)], buf1, sem).wait()
    pltpu.make_async_copy(o_hbm.at[pl.ds(0, tm)], buf2, sem).wait()
    wt = wt_ref[...]
    y = wt[:, 0:1] * buf1[...] + wt[:, 1:2] * buf2[...]
    r = DN_ALPHA * x_ref[...] + gate_ref[0] * y
    xo_ref[...] = _layer_norm(r, g_ref[...], b_ref[...])


def _combine_ln(o_sorted, d1, d2, wts, xf, gate, ln_g, ln_b, seq, tm=256):
    n, d = xf.shape
    row = lambda i, d1, d2: (i, 0)
    const = lambda i, d1, d2: (0, 0)
    return pl.pallas_call(
        functools.partial(_combine_ln_kernel, tm=tm),
        grid_spec=pltpu.PrefetchScalarGridSpec(
            num_scalar_prefetch=2,
            grid=(n // tm,),
            in_specs=[pl.BlockSpec(memory_space=pl.ANY),
                      pl.BlockSpec((tm, LANES), row),
                      pl.BlockSpec((tm, d), row),
                      pl.BlockSpec((1, 1, d), lambda i, d1, d2: (i * tm // seq, 0, 0)),
                      pl.BlockSpec((1, d), const),
                      pl.BlockSpec((1, d), const)],
            out_specs=pl.BlockSpec((tm, d), row),
            scratch_shapes=[pltpu.VMEM((tm, d), F32), pltpu.VMEM((tm, d), F32), pltpu.SemaphoreType.DMA(())]),
        out_shape=jax.ShapeDtypeStruct((n, d), F32),
        compiler_params=_cparams("arbitrary"),
    )(d1, d2, o_sorted, wts, xf, gate, ln_g.reshape(1, d), ln_b.reshape(1, d))


def kernel(x, c, w_ada, b_ada, ln_g, ln_b, gdn_w_in, gdn_conv_w, gdn_a_log, gdn_dt_bias, gdn_norm_w, gdn_w_out,
           kv_w_ada, kv_b_ada, sb_w_kv, sb_w_q, sb_w_out, ffn_w_gu, ffn_w_down, moe_w_router, moe_b_router,
           moe_w_gu, moe_w_down):
    bsz, seq, d = x.shape
    n = bsz * seq
    assert w_ada.shape[0] == DEPTH and gdn_w_in.shape[0] == 1 and sb_w_q.shape[0] == 1
    n_vh = gdn_a_log.shape[1]
    n_qk = n_vh // 2
    key_dim = n_qk * HEAD_DIM
    value_dim = n_vh * HEAD_DIM
    main_cols = 2 * key_dim + 2 * value_dim
    n_sb_heads = sb_w_q.shape[-1] // HEAD_DIM
    xf = x.reshape(n, d)

    c_pad = jnp.zeros((8, d), F32).at[:bsz].set(c)
    def mod_vectors(w, layer, b, parts):
        m = _ada(c_pad, w, layer, b)[:bsz]
        return [v.reshape(bsz, 1, d) for v in jnp.split(m, parts, axis=-1)]
    sh_m0, sc_m0, gt_m0, sh_f0, sc_f0, gt_f0 = mod_vectors(w_ada, 0, b_ada[0], 6)
    sh_m1, sc_m1, gt_m1, sh_f1, sc_f1, gt_f1 = mod_vectors(w_ada, 1, b_ada[1], 6)
    sh_kv, sc_kv = mod_vectors(kv_w_ada[None], 0, kv_b_ada, 2)

    h = _modulate(xf, sh_m0, sc_m0, seq)
    w_in_t = jnp.swapaxes(gdn_w_in, 1, 2)
    proj = _ws_mm(h, w_in_t, 0, main_cols, F32, transposed=True)
    gate_logits = _narrow_mm(h, w_in_t, 0, main_cols, 2 * n_vh)
    prm = jnp.zeros((8, LANES), F32)
    prm = prm.at[0, n_vh:2 * n_vh].set(gdn_dt_bias[0].astype(F32))
    prm = prm.at[1, n_vh:2 * n_vh].set(-jnp.exp(gdn_a_log[0].astype(F32)))
    conv_wt = jnp.transpose(gdn_conv_w[0])
    lg3 = gate_logits.reshape(bsz, seq, LANES)
    lgt = jnp.transpose(lg3[:, :, n_vh:2 * n_vh], (0, 2, 1)).reshape(bsz, n_vh, seq // GDN_CHUNK, GDN_CHUNK)
    prow = jnp.broadcast_to(prm[0:2, n_vh:2 * n_vh].T[:, :, None], (n_vh, 2, GDN_CHUNK))
    o = _gdn(proj.reshape(bsz, seq, main_cols), lg3, lgt, prm, prow, conv_wt,
             gdn_norm_w[0].reshape(1, HEAD_DIM), n_qk)
    x1, h1 = _proj_ln(o.reshape(n, value_dim), gdn_w_out[0].astype(BF16), xf, gt_m0, ln_g[0, 0], ln_b[0, 0],
                      [(sh_f0, sc_f0)], seq)

    hm = _ws_swiglu(h1, ffn_w_gu, 0)
    x2, h2, h_kv = _proj_ln(hm, ffn_w_down[0].astype(BF16), x1, gt_f0, ln_g[0, 1], ln_b[0, 1],
                            [(sh_m1, sc_m1), (sh_kv, sc_kv)], seq)

    kv = _ws_mm(h_kv, sb_w_kv[None], 0, 2 * n_sb_heads * HEAD_DIM, BF16)
    q = _ws_mm(h2, sb_w_q, 0, n_sb_heads * HEAD_DIM, BF16)
    oa = _sb_attention(q.reshape(bsz, seq, -1), kv.reshape(bsz, seq, -1), n_sb_heads)
    w_router = jnp.zeros((d, LANES), F32).at[:, :N_EXPERTS].set(moe_w_router[0])
    x3, h3, logits = _proj_ln(oa.reshape(n, -1), sb_w_out[0].astype(BF16), x2, gt_m1, ln_g[1, 0], ln_b[1, 0],
                              [(sh_f1, sc_f1)], seq, h_dtype=F32, w_router=w_router)

    b_router = jnp.zeros((1, LANES), F32).at[0, :N_EXPERTS].set(moe_b_router[0].astype(F32))
    idx, wts, cnt = _route(logits, b_router)
    tm_g = MOE_TILE
    n_tiles = 2 * n // tm_g + N_EXPERTS
    counts = cnt[0, :N_EXPERTS].astype(jnp.int32)
    tiles_per = (counts + tm_g - 1) // tm_g
    tile_end = jnp.cumsum(tiles_per)
    tile_first = tile_end - tiles_per
    row_off = tile_first * tm_g
    tile_ids = jnp.arange(n_tiles, dtype=jnp.int32)
    tile_expert = jnp.minimum(jnp.searchsorted(tile_end, tile_ids, side="right"), N_EXPERTS - 1).astype(jnp.int32)
    rows_left = counts[tile_expert] - (tile_ids - tile_first[tile_expert]) * tm_g
    tile_rows = jnp.where(tile_ids < tile_end[-1], jnp.clip(rows_left, 0, tm_g), 0)
    tile_steps = ((tile_rows + MOE_ROW_STEP - 1) // MOE_ROW_STEP).astype(jnp.int32)
    pieces = tm_g // MOE_ROW_STEP
    piece_used = (jnp.arange(pieces, dtype=jnp.int32)[None, :] < tile_steps[:, None]).astype(jnp.int32).reshape(-1)
    dest1 = row_off[idx[:, 0]] + idx[:, 2]
    dest2 = row_off[idx[:, 1]] + idx[:, 3]
    tok = jnp.arange(n, dtype=jnp.int32)
    src = jnp.zeros((n_tiles * tm_g,), jnp.int32).at[dest1].set(tok).at[dest2].set(tok)

    xs = _gather_rows(h3, src, piece_used)
    hm2 = _group_swiglu(xs, moe_w_gu, 0, tile_expert, tile_steps, tm_g)
    o_sorted = _group_down(hm2, moe_w_down[0], tile_expert, tile_steps, tm_g)
    out = _combine_ln(o_sorted, dest1, dest2, wts, x3, gt_f1, ln_g[1, 1], ln_b[1, 1], seq)
    return out.reshape(bsz, seq, d)
```

```python
import functools

import jax
import jax.numpy as jnp
from jax import lax
from jax.experimental import pallas as pl
from jax.experimental.pallas import tpu as pltpu

F32 = jnp.float32
BF16 = jnp.bfloat16

LANES = 128
VMEM_LIMIT_BYTES = 56 * 1024 * 1024

HEAD_DIM = 128
GDN_CHUNK = 64
CONV_WIDTH = 4
N_EXPERTS = 8
MOE_TILE = 1024
MOE_ROW_STEP = 256
LN_EPS = 1e-5
RMS_EPS = 1e-6
L2_EPS = 1e-6
DEPTH = 2
DN_ALPHA = (2.0 * DEPTH) ** 0.25


def _cparams(*sem):
    return pltpu.CompilerParams(dimension_semantics=sem, vmem_limit_bytes=VMEM_LIMIT_BYTES)


def _dot(a, b):
    return jnp.dot(a, b, preferred_element_type=F32)


def _dot_nt(a, b):
    return lax.dot_general(a, b, (((1,), (1,)), ((), ())), preferred_element_type=F32)


def _dot_tn(a, b):
    return lax.dot_general(a, b, (((0,), (0,)), ((), ())), preferred_element_type=F32)


def _split(a):
    hi = a.astype(BF16)
    lo = (a - hi.astype(F32)).astype(BF16)
    return hi, lo


def _dot3(a, b):
    ah, al = _split(a)
    bh, bl = _split(b)
    return _dot(ah, bh) + (_dot(ah, bl) + _dot(al, bh))


def _softplus(x):
    return jnp.maximum(x, 0.0) + jnp.log(1.0 + jnp.exp(-jnp.abs(x)))


def _silu(x):
    return x * jax.nn.sigmoid(x)


def _ada_kernel(c_ref, w_ref, b_ref, o_ref):
    s = _silu(c_ref[...]).astype(BF16)
    o_ref[...] = _dot(s, w_ref[...].astype(BF16)) + b_ref[...]


def _ada(c_pad, w, layer, b, tn=1024):
    _, d, n_out = w.shape
    return pl.pallas_call(
        _ada_kernel,
        grid=(n_out // tn,),
        in_specs=[pl.BlockSpec((8, d), lambda n: (0, 0)),
                  pl.BlockSpec((None, d, tn), lambda n: (layer, 0, n)),
                  pl.BlockSpec((1, tn), lambda n: (0, n))],
        out_specs=pl.BlockSpec((8, tn), lambda n: (0, n)),
        out_shape=jax.ShapeDtypeStruct((8, n_out), F32),
        compiler_params=_cparams("arbitrary"),
    )(c_pad, w, b.reshape(1, n_out))


def _modulate_kernel(x_ref, sh_ref, sc_ref, o_ref):
    o_ref[...] = (x_ref[...] * (1.0 + sc_ref[0]) + sh_ref[0]).astype(o_ref.dtype)


def _modulate(xf, shift, scale, seq, tm=512):
    n, d = xf.shape
    bmap = lambda i: (i * tm // seq, 0, 0)
    return pl.pallas_call(
        _modulate_kernel,
        grid=(n // tm,),
        in_specs=[pl.BlockSpec((tm, d), lambda i: (i, 0)),
                  pl.BlockSpec((1, 1, d), bmap),
                  pl.BlockSpec((1, 1, d), bmap)],
        out_specs=pl.BlockSpec((tm, d), lambda i: (i, 0)),
        out_shape=jax.ShapeDtypeStruct((n, d), BF16),
        compiler_params=_cparams("arbitrary"),
    )(xf, shift, scale)


def _ws_mm_kernel(a_ref, w_ref, o_ref, wb_ref, *, transposed):
    @pl.when(pl.program_id(1) == 0)
    def _():
        w = w_ref[...]
        wb_ref[...] = (w.T if transposed else w).astype(BF16)

    o_ref[...] = _dot(a_ref[...], wb_ref[...]).astype(o_ref.dtype)


def _ws_mm(a, w, layer, n_out, out_dtype, transposed=False, tm=1024, tn=1024):
    m, k = a.shape
    tn = min(tn, n_out)
    if transposed:
        w_spec = pl.BlockSpec((None, tn, k), lambda n, i: (layer, n, 0))
    else:
        w_spec = pl.BlockSpec((None, k, tn), lambda n, i: (layer, 0, n))
    return pl.pallas_call(
        functools.partial(_ws_mm_kernel, transposed=transposed),
        grid=(n_out // tn, m // tm),
        in_specs=[pl.BlockSpec((tm, k), lambda n, i: (i, 0)), w_spec],
        out_specs=pl.BlockSpec((tm, tn), lambda n, i: (i, n)),
        out_shape=jax.ShapeDtypeStruct((m, n_out), out_dtype),
        scratch_shapes=[pltpu.VMEM((k, tn), BF16)],
        compiler_params=_cparams("arbitrary", "arbitrary"),
    )(a, w)


def _narrow_mm_kernel(a_ref, w_hbm, o_ref, wf_ref, wb_ref, sem, *, layer, row0, n_rows):
    @pl.when(pl.program_id(0) == 0)
    def _():
        copy = pltpu.make_async_copy(w_hbm.at[layer, pl.ds(row0, n_rows), :], wf_ref, sem)
        copy.start()
        copy.wait()
        wb_ref[...] = jnp.zeros_like(wb_ref)
        wb_ref[:n_rows, :] = wf_ref[...].astype(BF16)

    o_ref[...] = _dot_nt(a_ref[...], wb_ref[...])


def _narrow_mm(a, w_t, layer, row0, n_rows, tm=1024):
    m, k = a.shape
    return pl.pallas_call(
        functools.partial(_narrow_mm_kernel, layer=layer, row0=row0, n_rows=n_rows),
        grid=(m // tm,),
        in_specs=[pl.BlockSpec((tm, k), lambda i: (i, 0)),
                  pl.BlockSpec(memory_space=pl.ANY)],
        out_specs=pl.BlockSpec((tm, LANES), lambda i: (i, 0)),
        out_shape=jax.ShapeDtypeStruct((m, LANES), F32),
        scratch_shapes=[pltpu.VMEM((n_rows, k), F32), pltpu.VMEM((LANES, k), BF16), pltpu.SemaphoreType.DMA(())],
        compiler_params=_cparams("arbitrary"),
    )(a, w_t)


def _ws_swiglu_kernel(a_ref, wg_ref, wu_ref, o_ref, wgb_ref, wub_ref):
    @pl.when(pl.program_id(1) == 0)
    def _():
        wgb_ref[...] = wg_ref[...].astype(BF16)
        wub_ref[...] = wu_ref[...].astype(BF16)

    a = a_ref[...]
    g = _dot(a, wgb_ref[...])
    u = _dot(a, wub_ref[...])
    o_ref[...] = (_silu(g) * u).astype(o_ref.dtype)


def _ws_swiglu(a, w_gu, layer, tm=1024, tn=512):
    m, k = a.shape
    f = w_gu.shape[-1] // 2
    nt = f // tn
    return pl.pallas_call(
        _ws_swiglu_kernel,
        grid=(nt, m // tm),
        in_specs=[pl.BlockSpec((tm, k), lambda n, i: (i, 0)),
                  pl.BlockSpec((None, k, tn), lambda n, i: (layer, 0, n)),
                  pl.BlockSpec((None, k, tn), lambda n, i: (layer, 0, nt + n))],
        out_specs=pl.BlockSpec((tm, tn), lambda n, i: (i, n)),
        out_shape=jax.ShapeDtypeStruct((m, f), BF16),
        scratch_shapes=[pltpu.VMEM((k, tn), BF16), pltpu.VMEM((k, tn), BF16)],
        compiler_params=_cparams("arbitrary", "arbitrary"),
    )(a, w_gu, w_gu)


def _layer_norm(r, g, b):
    mu = jnp.mean(r, axis=-1, keepdims=True)
    rc = r - mu
    var = jnp.mean(rc * rc, axis=-1, keepdims=True)
    return rc * lax.rsqrt(var + LN_EPS) * g + b


def _proj_ln_kernel(*refs, n_mod, nk, with_router):
    a_ref, w_ref, x_ref, gate_ref, g_ref, b_ref = refs[:6]
    mod_refs = refs[6:6 + 2 * n_mod]
    pos = 6 + 2 * n_mod
    wr_ref = refs[pos] if with_router else None
    pos += int(with_router)
    xo_ref = refs[pos]
    h_refs = refs[pos + 1:pos + 1 + n_mod]
    pos += 1 + n_mod
    lg_ref = refs[pos] if with_router else None
    acc_ref = refs[-1]
    kk = pl.program_id(1)

    @pl.when(kk == 0)
    def _():
        acc_ref[...] = jnp.zeros_like(acc_ref)

    acc_ref[...] += _dot(a_ref[...], w_ref[...])

    @pl.when(kk == nk - 1)
    def _():
        r = DN_ALPHA * x_ref[...] + gate_ref[0] * acc_ref[...]
        xn = _layer_norm(r, g_ref[...], b_ref[...])
        xo_ref[...] = xn
        for t in range(n_mod):
            h = xn * (1.0 + mod_refs[2 * t + 1][0]) + mod_refs[2 * t][0]
            h_refs[t][...] = h.astype(h_refs[t].dtype)
            if with_router and t == 0:
                lg_ref[...] = _dot3(h, wr_ref[...])


def _k_tile(k, limit=2048, unit=256):
    nk = 1
    while k % nk or k // nk > limit or (k // nk) % unit:
        nk += 1
    return k // nk


def _proj_ln(a, w, xf, gate, ln_g, ln_b, mods, seq, h_dtype=BF16, w_router=None, tm=512):
    m, k = a.shape
    d = w.shape[-1]
    tk = _k_tile(k)
    nk = k // tk
    n_mod = len(mods)
    with_router = w_router is not None
    bmap = lambda i, kk: (i * tm // seq, 0, 0)
    row = lambda i, kk: (i, 0)
    const = lambda i, kk: (0, 0)
    in_specs = [pl.BlockSpec((tm, tk), lambda i, kk: (i, kk)),
                pl.BlockSpec((tk, d), lambda i, kk: (kk, 0)),
                pl.BlockSpec((tm, d), row),
                pl.BlockSpec((1, 1, d), bmap),
                pl.BlockSpec((1, d), const),
                pl.BlockSpec((1, d), const)]
    args = [a, w, xf, gate, ln_g.reshape(1, d), ln_b.reshape(1, d)]
    for sh, sc in mods:
        in_specs += [pl.BlockSpec((1, 1, d), bmap), pl.BlockSpec((1, 1, d), bmap)]
        args += [sh, sc]
    out_specs = [pl.BlockSpec((tm, d), row)] + [pl.BlockSpec((tm, d), row)] * n_mod
    out_shape = [jax.ShapeDtypeStruct((m, d), F32)] + [jax.ShapeDtypeStruct((m, d), h_dtype)] * n_mod
    if with_router:
        in_specs.append(pl.BlockSpec((d, LANES), const))
        args.append(w_router)
        out_specs.append(pl.BlockSpec((tm, LANES), row))
        out_shape.append(jax.ShapeDtypeStruct((m, LANES), F32))
    return pl.pallas_call(
        functools.partial(_proj_ln_kernel, n_mod=n_mod, nk=nk, with_router=with_router),
        grid=(m // tm, nk),
        in_specs=in_specs,
        out_specs=out_specs,
        out_shape=out_shape,
        scratch_shapes=[pltpu.VMEM((tm, d), F32)],
        compiler_params=_cparams("arbitrary", "arbitrary"),
    )(*args)


GDN_GROUP = 8


def _gdn_kernel(q_ref, k_ref, v_ref, z_ref, cwq_ref, cwk_ref, cwv_ref, lg_ref, lgt_ref, prm_ref, prow_ref, nw_ref,
                o_ref, gcr_s, kw_r, nn_r, qp_r, op_r, dg_r):
    j = pl.program_id(1)
    seq = q_ref.shape[1]
    c = GDN_CHUNK
    hd = HEAD_DIM
    n_groups = seq // (c * GDN_GROUP)
    n_vh = 2 * pl.num_programs(1)
    halo = 8

    def window(ref, ch):
        if isinstance(ch, int) and ch == 0:
            body = ref[0, 0:c, :]
            return jnp.concatenate([jnp.zeros((halo, body.shape[1]), F32), body], axis=0)
        if isinstance(ch, int):
            return ref[0, ch * c - halo:(ch + 1) * c, :]
        return ref[0, pl.ds(pl.multiple_of(ch * c - halo, halo), c + halo), :]

    def conv_silu(xw, cw):
        acc = xw[halo:, :] * cw[CONV_WIDTH - 1:CONV_WIDTH, :]
        for s in range(1, CONV_WIDTH):
            acc = acc + xw[halo - s:halo - s + c, :] * cw[CONV_WIDTH - 1 - s:CONV_WIDTH - s, :]
        return _silu(acc)

    def l2norm(x):
        return x * lax.rsqrt(jnp.sum(x * x, axis=-1, keepdims=True) + L2_EPS)

    ri = lax.broadcasted_iota(jnp.int32, (c, c), 0)
    ci = lax.broadcasted_iota(jnp.int32, (c, c), 1)
    upper = (ri <= ci).astype(BF16)
    for hh in range(2):
        g_row = prow_ref[hh, 1:2, :] * _softplus(lgt_ref[0, hh] + prow_ref[hh, 0:1, :])
        g1, g2 = _split(g_row)
        g3 = (g_row - g1.astype(F32) - g2.astype(F32)).astype(BF16)
        gcr_s[hh] = _dot(g1, upper) + (_dot(g2, upper) + _dot(g3, upper))

    incl = ri >= ci
    strict = ri > ci
    lane = lax.broadcasted_iota(jnp.int32, (c, LANES), 1)
    pos = lax.broadcasted_iota(jnp.int32, (c, 1), 0)

    def select_lane(x, idx):
        return jnp.sum(jnp.where(lane == idx, x, 0.0), axis=1, keepdims=True)

    def prepare_group(first_chunk, slot):
        chunks = []
        for k in range(GDN_GROUP):
            ch = first_chunk + k
            rows = pl.ds(ch * c, c) if isinstance(ch, int) else pl.ds(pl.multiple_of(ch * c, c), c)
            qc = l2norm(conv_silu(window(q_ref, ch), cwq_ref[...])) * (hd ** -0.5)
            kc = l2norm(conv_silu(window(k_ref, ch), cwk_ref[...]))
            vc = conv_silu(window(v_ref, ch), cwv_ref[...])
            chunks.append((ch, rows, kc, qc, vc, kc.astype(BF16), qc.astype(BF16)))
        kks = [_dot_nt(kcb, kcb) for (_, _, _, _, _, kcb, _) in chunks]
        qks = [_dot_nt(qcb, kcb) for (_, _, _, _, _, kcb, qcb) in chunks]
        probs = []
        for (ch, rows, kc, qc, vc, _, _), kk, qk in zip(chunks, kks, qks):
            lgc = lg_ref[0, rows, :]
            gac = prm_ref[1:2, :] * _softplus(lgc + prm_ref[0:1, :])
            for hh in range(2):
                head = 2 * j + hh
                beta = jax.nn.sigmoid(select_lane(lgc, head))
                gc = jnp.broadcast_to(select_lane(gac, n_vh + head), (c, hd))
                s = 1
                while s < c:
                    gc = gc + jnp.where(pos >= s, pltpu.roll(gc, s, axis=0), 0.0)
                    s *= 2
                gl = jnp.broadcast_to(gc[c - 1:c, :], (c, hd))
                eg = jnp.exp(gc)
                row_b = jnp.broadcast_to(gcr_s[hh, pl.ds(ch, 1), :], (c, c))
                dec = jnp.exp(jnp.where(incl, gc[:, :c] - row_b, -jnp.inf))
                low = jnp.where(strict, kk * beta * dec, 0.0)
                rhs = jnp.concatenate([vc[:, hh * hd:(hh + 1) * hd] * beta, kc * beta * eg], axis=1)
                probs.append(dict(low=low, rhs=rhs, at=(qk * dec).astype(BF16), qd=qc * eg,
                                  kd=(kc * jnp.exp(gl - gc)).astype(BF16), dg=jnp.exp(gl[:8, :])))
        yield
        eye = (ri == ci).astype(F32)
        ps = [p["low"] for p in probs]
        xs = [eye - p["low"] for p in probs]
        n_terms = 2
        while n_terms < c:
            pbs = [p.astype(BF16) for p in ps]
            ps = [_dot(pb, pb) for pb in pbs]
            xs = [x + _dot(x.astype(BF16), p.astype(BF16)) for x, p in zip(xs, ps)]
            n_terms *= 2
            if n_terms in (4, 16, 64):
                yield
        sols = [_dot(x.astype(BF16), p["rhs"].astype(BF16)) for x, p in zip(xs, probs)]
        ubs = [sol[:, :hd].astype(BF16) for sol in sols]
        wbs = [sol[:, hd:].astype(BF16) for sol in sols]
        kws = [_dot_tn(p["kd"], wb) for p, wb in zip(probs, wbs)]
        nns = [_dot_tn(p["kd"], ub) for p, ub in zip(probs, ubs)]
        aws = [_dot(p["at"], wb) for p, wb in zip(probs, wbs)]
        ops = [_dot(p["at"], ub) for p, ub in zip(probs, ubs)]
        for n, p in enumerate(probs):
            k, hh = divmod(n, 2)
            kw_r[slot, k, hh] = kws[n].astype(BF16)
            nn_r[slot, k, hh] = nns[n]
            qp_r[slot, k, hh] = (p["qd"] - aws[n]).astype(BF16)
            op_r[slot, k, hh] = ops[n]
            dg_r[slot, k, hh] = p["dg"]

    def scan(ch, slot, k, states):
        rows = pl.ds(pl.multiple_of(ch * c, c), c)
        new_states = []
        for hh in range(2):
            st = states[hh]
            sb = st.astype(BF16)
            o = _dot(qp_r[slot, k, hh], sb) + op_r[slot, k, hh]
            decay = jnp.broadcast_to(dg_r[slot, k, hh][0:1, :], (hd, hd))
            new_states.append(st * decay - _dot(kw_r[slot, k, hh], sb) + nn_r[slot, k, hh])
            o = o * lax.rsqrt(jnp.mean(o * o, axis=-1, keepdims=True) + RMS_EPS) * nw_ref[...]
            cols = slice(hh * hd, (hh + 1) * hd)
            o_ref[0, rows, cols] = (o * _silu(z_ref[0, rows, cols])).astype(o_ref.dtype)
        return tuple(new_states)

    for _ in prepare_group(0, 0):
        pass

    def body(g, states):
        slot = g % 2
        stages = prepare_group((g + 1) * GDN_GROUP, 1 - slot)
        for k in range(GDN_GROUP):
            if k % (GDN_GROUP // 4) == 0:
                next(stages)
            states = scan(g * GDN_GROUP + k, slot, k, states)
        for _ in stages:
            pass
        return states

    zero = jnp.zeros((hd, hd), F32)
    states = lax.fori_loop(0, n_groups - 1, body, (zero, zero))
    last = n_groups - 1
    for k in range(GDN_GROUP):
        states = scan(last * GDN_GROUP + k, last % 2, k, states)


def _gdn(proj, lg, lgt, prm, prow, conv_wt, norm_w, n_qk_heads):
    bsz, seq, _ = proj.shape
    hq = n_qk_heads
    hd = HEAD_DIM
    n_chunks = seq // GDN_CHUNK
    ring = (2, GDN_GROUP, 2)
    return pl.pallas_call(
        _gdn_kernel,
        grid=(bsz, hq),
        in_specs=[pl.BlockSpec((1, seq, hd), lambda b, j: (b, 0, j)),
                  pl.BlockSpec((1, seq, hd), lambda b, j: (b, 0, hq + j)),
                  pl.BlockSpec((1, seq, 2 * hd), lambda b, j: (b, 0, hq + j)),
                  pl.BlockSpec((1, seq, 2 * hd), lambda b, j: (b, 0, 2 * hq + j)),
                  pl.BlockSpec((CONV_WIDTH, hd), lambda b, j: (0, j)),
                  pl.BlockSpec((CONV_WIDTH, hd), lambda b, j: (0, hq + j)),
                  pl.BlockSpec((CONV_WIDTH, 2 * hd), lambda b, j: (0, hq + j)),
                  pl.BlockSpec((1, seq, LANES), lambda b, j: (b, 0, 0)),
                  pl.BlockSpec((1, 2, n_chunks, GDN_CHUNK), lambda b, j: (b, j, 0, 0)),
                  pl.BlockSpec((8, LANES), lambda b, j: (0, 0)),
                  pl.BlockSpec((2, 2, GDN_CHUNK), lambda b, j: (j, 0, 0)),
                  pl.BlockSpec((1, hd), lambda b, j: (0, 0))],
        out_specs=pl.BlockSpec((1, seq, 2 * hd), lambda b, j: (b, 0, j)),
        out_shape=jax.ShapeDtypeStruct((bsz, seq, 2 * hq * hd), BF16),
        scratch_shapes=[pltpu.VMEM((2, n_chunks, GDN_CHUNK), F32),
                        pltpu.VMEM(ring + (hd, hd), BF16),
                        pltpu.VMEM(ring + (hd, hd), F32),
                        pltpu.VMEM(ring + (GDN_CHUNK, hd), BF16),
                        pltpu.VMEM(ring + (GDN_CHUNK, hd), F32),
                        pltpu.VMEM(ring + (8, hd), F32)],
        compiler_params=_cparams("arbitrary", "arbitrary"),
    )(proj, proj, proj, proj, conv_wt, conv_wt, conv_wt, lg, lgt, prm, prow, norm_w)


SB_HEADS_PER_STEP = 4
SB_ZERO_EXPONENT = -110.0


def _sb_kernel(q_ref, k_ref, v_ref, o_ref, kmax_s, *, blk):
    i = pl.program_id(2)
    hd = HEAD_DIM
    scale = hd ** -0.5
    heads = range(SB_HEADS_PER_STEP)

    @pl.when(i == 0)
    def _():
        for hh in heads:
            kf = k_ref[0, :, hh * hd:(hh + 1) * hd].astype(F32)
            n2 = jnp.max(jnp.sum(kf * kf, axis=1, keepdims=True), axis=0, keepdims=True)
            kmax_s[hh] = jnp.broadcast_to(jnp.sqrt(n2) * scale, kmax_s.shape[1:])

    ri = lax.broadcasted_iota(jnp.int32, (blk, blk), 0)
    ci = lax.broadcasted_iota(jnp.int32, (blk, blk), 1)
    tri = (ri >= ci).astype(BF16)
    causal = ci < ri
    qs = [q_ref[0, :, hh * hd:(hh + 1) * hd] for hh in heads]
    z_bound = []
    for hh in heads:
        qf = qs[hh].astype(F32)
        z_bound.append(jnp.sqrt(jnp.sum(qf * qf, axis=1, keepdims=True)) * kmax_s[hh][0:1, 0:1])

    def block(jb, accs, runs, diagonal):
        rows = pl.ds(pl.multiple_of(jb * blk, blk), blk)
        cols = [slice(hh * hd, (hh + 1) * hd) for hh in heads]
        zs = [_dot_nt(qs[hh], k_ref[0, rows, cols[hh]]) * scale for hh in heads]
        lss = [-_softplus(z) for z in zs]
        if diagonal:
            lss = [jnp.where(causal, ls, 0.0) for ls in lss]
        parts = [_split(ls) for ls in lss]
        rcs = [_dot(hi, tri) + _dot(lo, tri) for hi, lo in parts]
        ws = [jnp.exp(zs[hh] + rcs[hh] + runs[hh]) for hh in heads]
        if diagonal:
            ws = [jnp.where(causal, w, 0.0) for w in ws]
        new_accs = [accs[hh] + _dot(ws[hh].astype(BF16), v_ref[0, rows, cols[hh]]) for hh in heads]
        new_runs = [runs[hh] + rcs[hh][:, :1] for hh in heads]
        return tuple(new_accs), tuple(new_runs)

    def any_weight_left(runs):
        m = jnp.max(runs[0] + z_bound[0])
        for hh in heads[1:]:
            m = jnp.maximum(m, jnp.max(runs[hh] + z_bound[hh]))
        return (m > SB_ZERO_EXPONENT).astype(jnp.int32)

    zeros = tuple(jnp.zeros((blk, hd), F32) for _ in heads)
    zrun = tuple(jnp.zeros((blk, 1), F32) for _ in heads)
    accs, runs = block(i, zeros, zrun, True)

    def cond(carry):
        return jnp.logical_and(carry[0] >= 0, carry[1] > 0)

    def body(carry):
        jb, _, accs, runs = carry
        accs, runs = block(jb, accs, runs, False)
        return jb - 1, any_weight_left(runs), accs, runs

    _, _, accs, _ = lax.while_loop(cond, body, (i - 1, any_weight_left(runs), accs, runs))
    for hh in heads:
        o_ref[0, :, hh * hd:(hh + 1) * hd] = accs[hh].astype(o_ref.dtype)


def _sb_attention(q, kv, n_heads, blk=256):
    bsz, seq, _ = q.shape
    w = SB_HEADS_PER_STEP * HEAD_DIM
    n_hp = n_heads // SB_HEADS_PER_STEP
    return pl.pallas_call(
        functools.partial(_sb_kernel, blk=blk),
        grid=(bsz, n_hp, seq // blk),
        in_specs=[pl.BlockSpec((1, blk, w), lambda b, h, i: (b, i, h)),
                  pl.BlockSpec((1, seq, w), lambda b, h, i: (b, 0, h)),
                  pl.BlockSpec((1, seq, w), lambda b, h, i: (b, 0, n_hp + h))],
        out_specs=pl.BlockSpec((1, blk, w), lambda b, h, i: (b, i, h)),
        out_shape=jax.ShapeDtypeStruct((bsz, seq, n_heads * HEAD_DIM), BF16),
        scratch_shapes=[pltpu.VMEM((SB_HEADS_PER_STEP, 8, LANES), F32)],
        compiler_params=_cparams("arbitrary", "arbitrary", "arbitrary"),
    )(q, kv, kv)


def _route_kernel(lg_ref, br_ref, idx_ref, wt_ref, cnt_ref, carry_ref, *, tm):
    i = pl.program_id(0)

    @pl.when(i == 0)
    def _():
        carry_ref[...] = jnp.zeros_like(carry_ref)

    lane = lax.broadcasted_iota(jnp.int32, (tm, LANES), 1)
    lane_f = lane.astype(F32)
    logits = jnp.where(lane < N_EXPERTS, lg_ref[...] + br_ref[...], -jnp.inf)
    m1 = jnp.max(logits, axis=1, keepdims=True)
    i1 = jnp.min(jnp.where(logits == m1, lane_f, float(LANES)), axis=1, keepdims=True)
    oh1 = lane_f == i1
    rest = jnp.where(oh1, -jnp.inf, logits)
    m2 = jnp.max(rest, axis=1, keepdims=True)
    i2 = jnp.min(jnp.where(rest == m2, lane_f, float(LANES)), axis=1, keepdims=True)
    oh2 = lane_f == i2
    e21 = jnp.exp(m2 - m1)
    w1 = 1.0 / (1.0 + e21)
    w2 = e21 / (1.0 + e21)

    cnt = jnp.where(oh1, 1.0, jnp.where(oh2, 1.0, 0.0))
    before = (lax.broadcasted_iota(jnp.int32, (tm, tm), 1) < lax.broadcasted_iota(jnp.int32, (tm, tm), 0)).astype(BF16)
    rank = _dot(before, cnt.astype(BF16)) + carry_ref[0:1, :]
    r1 = jnp.sum(jnp.where(oh1, rank, 0.0), axis=1, keepdims=True)
    r2 = jnp.sum(jnp.where(oh2, rank, 0.0), axis=1, keepdims=True)
    total = carry_ref[0:1, :] + jnp.sum(cnt, axis=0, keepdims=True)
    carry_ref[...] = jnp.broadcast_to(total, carry_ref.shape)
    cnt_ref[...] = jnp.broadcast_to(total, cnt_ref.shape)

    packed = jnp.where(lane == 0, i1, jnp.where(lane == 1, i2, jnp.where(lane == 2, r1, jnp.where(lane == 3, r2, 0.0))))
    idx_ref[...] = packed.astype(jnp.int32)
    wt_ref[...] = jnp.where(lane == 0, w1, jnp.where(lane == 1, w2, 0.0))


def _route(logits, b_router_pad, tm=512):
    n = logits.shape[0]
    row = lambda i: (i, 0)
    return pl.pallas_call(
        functools.partial(_route_kernel, tm=tm),
        grid=(n // tm,),
        in_specs=[pl.BlockSpec((tm, LANES), row), pl.BlockSpec((1, LANES), lambda i: (0, 0))],
        out_specs=[pl.BlockSpec((tm, LANES), row), pl.BlockSpec((tm, LANES), row),
                   pl.BlockSpec((8, LANES), lambda i: (0, 0))],
        out_shape=[jax.ShapeDtypeStruct((n, LANES), jnp.int32), jax.ShapeDtypeStruct((n, LANES), F32),
                   jax.ShapeDtypeStruct((8, LANES), F32)],
        scratch_shapes=[pltpu.VMEM((8, LANES), F32)],
        compiler_params=_cparams("arbitrary"),
    )(logits, b_router_pad)


def _row_copy(src_hbm, src_row, dst, dst_row, sem):
    return pltpu.make_async_copy(src_hbm.at[pl.ds(src_row, 1)], dst.at[pl.ds(dst_row, 1)], sem)


def _gather_kernel(src_ref, used_ref, h_hbm, o_ref, buf, sem, *, tg):
    i = pl.program_id(0)
    base = i * tg

    @pl.when(used_ref[i] > 0)
    def _():
        def start(r, carry):
            _row_copy(h_hbm, src_ref[base + r], buf, r, sem).start()
            return carry

        lax.fori_loop(0, tg, start, 0, unroll=8)
        pltpu.make_async_copy(h_hbm.at[pl.ds(0, tg)], buf, sem).wait()
        o_ref[...] = buf[...].astype(o_ref.dtype)

    @pl.when(used_ref[i] == 0)
    def _():
        o_ref[...] = jnp.zeros_like(o_ref)


def _gather_rows(h, src, piece_used):
    tg = MOE_ROW_STEP
    m_pad = src.shape[0]
    d = h.shape[1]
    return pl.pallas_call(
        functools.partial(_gather_kernel, tg=tg),
        grid_spec=pltpu.PrefetchScalarGridSpec(
            num_scalar_prefetch=2,
            grid=(m_pad // tg,),
            in_specs=[pl.BlockSpec(memory_space=pl.ANY)],
            out_specs=pl.BlockSpec((tg, d), lambda i, src, used: (i, 0)),
            scratch_shapes=[pltpu.VMEM((tg, d), h.dtype), pltpu.SemaphoreType.DMA(())]),
        out_shape=jax.ShapeDtypeStruct((m_pad, d), BF16),
        compiler_params=_cparams("arbitrary"),
    )(src, piece_used, h)


def _group_swiglu_kernel(te_ref, tv_ref, a_ref, wg_ref, wu_ref, o_ref, wgb_ref, wub_ref):
    i = pl.program_id(1)
    new_expert = jnp.logical_or(i == 0, te_ref[i] != te_ref[jnp.maximum(i - 1, 0)])

    @pl.when(new_expert)
    def _():
        wgb_ref[...] = wg_ref[...].astype(BF16)
        wub_ref[...] = wu_ref[...].astype(BF16)

    tm = a_ref.shape[0]
    n_steps = tv_ref[i]

    def compute(n_rows):
        a = a_ref[:n_rows, :]
        g = _dot(a, wgb_ref[...])
        u = _dot(a, wub_ref[...])
        o_ref[:n_rows, :] = (_silu(g) * u).astype(o_ref.dtype)
        if n_rows < tm:
            o_ref[n_rows:, :] = jnp.zeros((tm - n_rows, o_ref.shape[1]), o_ref.dtype)

    for s in range(1, tm // MOE_ROW_STEP + 1):
        pl.when(n_steps == s)(functools.partial(compute, s * MOE_ROW_STEP))

    @pl.when(n_steps == 0)
    def _():
        o_ref[...] = jnp.zeros_like(o_ref)


def _group_swiglu(xs, w_gu, layer, tile_expert, tile_valid, tm, tn=512):
    m_pad, k = xs.shape
    f = w_gu.shape[-1] // 2
    nt = f // tn
    return pl.pallas_call(
        _group_swiglu_kernel,
        grid_spec=pltpu.PrefetchScalarGridSpec(
            num_scalar_prefetch=2,
            grid=(nt, m_pad // tm),
            in_specs=[pl.BlockSpec((tm, k), lambda n, i, te, tv: (jnp.where(tv[i] > 0, i, 0), 0)),
                      pl.BlockSpec((None, None, k, tn), lambda n, i, te, tv: (layer, te[i], 0, n)),
                      pl.BlockSpec((None, None, k, tn), lambda n, i, te, tv: (layer, te[i], 0, nt + n))],
            out_specs=pl.BlockSpec((tm, tn), lambda n, i, te, tv: (i, n)),
            scratch_shapes=[pltpu.VMEM((k, tn), BF16), pltpu.VMEM((k, tn), BF16)]),
        out_shape=jax.ShapeDtypeStruct((m_pad, f), BF16),
        compiler_params=_cparams("arbitrary", "arbitrary"),
    )(tile_expert, tile_valid, xs, w_gu, w_gu)


def _group_down_kernel(te_ref, tv_ref, a_ref, w_ref, o_ref, *, nk):
    i = pl.program_id(0)
    kk = pl.program_id(1)

    @pl.when(kk == 0)
    def _():
        o_ref[...] = jnp.zeros_like(o_ref)

    n_steps = tv_ref[i]

    def accumulate(n_rows):
        o_ref[:n_rows, :] += _dot(a_ref[:n_rows, :], w_ref[...].astype(BF16))

    for s in range(1, a_ref.shape[0] // MOE_ROW_STEP + 1):
        pl.when(n_steps == s)(functools.partial(accumulate, s * MOE_ROW_STEP))


def _group_down(hm, w_down, tile_expert, tile_valid, tm, tk=1024):
    m_pad, f = hm.shape
    d = w_down.shape[-1]
    nk = f // tk
    return pl.pallas_call(
        functools.partial(_group_down_kernel, nk=nk),
        grid_spec=pltpu.PrefetchScalarGridSpec(
            num_scalar_prefetch=2,
            grid=(m_pad // tm, nk),
            in_specs=[pl.BlockSpec((tm, tk), lambda i, kk, te, tv: (i, jnp.where(tv[i] > 0, kk, 0))),
                      pl.BlockSpec((None, tk, d), lambda i, kk, te, tv: (te[i], jnp.where(tv[i] > 0, kk, 0), 0))],
            out_specs=pl.BlockSpec((tm, d), lambda i, kk, te, tv: (i, 0))),
        out_shape=jax.ShapeDtypeStruct((m_pad, d), F32),
        compiler_params=_cparams("arbitrary", "arbitrary"),
    )(tile_expert, tile_valid, hm, w_down)


def _combine_ln_kernel(d1_ref, d2_ref, o_hbm, wt_ref, x_ref, gate_ref, g_ref, b_ref, xo_ref, buf1, buf2, sem, *, tm):
    base = pl.program_id(0) * tm

    def start(r, carry):
        _row_copy(o_hbm, d1_ref[base + r], buf1, r, sem).start()
        _row_copy(o_hbm, d2_ref[base + r], buf2, r, sem).start()
        return carry

    lax.fori_loop(0, tm, start, 0, unroll=8)
    pltpu.make_async_copy(o_hbm.at[pl.ds(0, tm)], buf1, sem).wait()
    pltpu.make_async_copy(o_hbm.at[pl.ds(0, tm)], buf2, sem).wait()
    wt = wt_ref[...]
    y = wt[:, 0:1] * buf1[...] + wt[:, 1:2] * buf2[...]
    r = DN_ALPHA * x_ref[...] + gate_ref[0] * y
    xo_ref[...] = _layer_norm(r, g_ref[...], b_ref[...])


def _combine_ln(o_sorted, d1, d2, wts, xf, gate, ln_g, ln_b, seq, tm=256):
    n, d = xf.shape
    row = lambda i, d1, d2: (i, 0)
    const = lambda i, d1, d2: (0, 0)
    return pl.pallas_call(
        functools.partial(_combine_ln_kernel, tm=tm),
        grid_spec=pltpu.PrefetchScalarGridSpec(
            num_scalar_prefetch=2,
            grid=(n // tm,),
            in_specs=[pl.BlockSpec(memory_space=pl.ANY),
                      pl.BlockSpec((tm, LANES), row),
                      pl.BlockSpec((tm, d), row),
                      pl.BlockSpec((1, 1, d), lambda i, d1, d2: (i * tm // seq, 0, 0)),
                      pl.BlockSpec((1, d), const),
                      pl.BlockSpec((1, d), const)],
            out_specs=pl.BlockSpec((tm, d), row),
            scratch_shapes=[pltpu.VMEM((tm, d), F32), pltpu.VMEM((tm, d), F32), pltpu.SemaphoreType.DMA(())]),
        out_shape=jax.ShapeDtypeStruct((n, d), F32),
        compiler_params=_cparams("arbitrary"),
    )(d1, d2, o_sorted, wts, xf, gate, ln_g.reshape(1, d), ln_b.reshape(1, d))


def kernel(x, c, w_ada, b_ada, ln_g, ln_b, gdn_w_in, gdn_conv_w, gdn_a_log, gdn_dt_bias, gdn_norm_w, gdn_w_out,
           kv_w_ada, kv_b_ada, sb_w_kv, sb_w_q, sb_w_out, ffn_w_gu, ffn_w_down, moe_w_router, moe_b_router,
           moe_w_gu, moe_w_down):
    bsz, seq, d = x.shape
    n = bsz * seq
    assert w_ada.shape[0] == DEPTH and gdn_w_in.shape[0] == 1 and sb_w_q.shape[0] == 1
    n_vh = gdn_a_log.shape[1]
    n_qk = n_vh // 2
    key_dim = n_qk * HEAD_DIM
    value_dim = n_vh * HEAD_DIM
    main_cols = 2 * key_dim + 2 * value_dim
    n_sb_heads = sb_w_q.shape[-1] // HEAD_DIM
    xf = x.reshape(n, d)

    c_pad = jnp.zeros((8, d), F32).at[:bsz].set(c)
    def mod_vectors(w, layer, b, parts):
        m = _ada(c_pad, w, layer, b)[:bsz]
        return [v.reshape(bsz, 1, d) for v in jnp.split(m, parts, axis=-1)]
    sh_m0, sc_m0, gt_m0, sh_f0, sc_f0, gt_f0 = mod_vectors(w_ada, 0, b_ada[0], 6)
    sh_m1, sc_m1, gt_m1, sh_f1, sc_f1, gt_f1 = mod_vectors(w_ada, 1, b_ada[1], 6)
    sh_kv, sc_kv = mod_vectors(kv_w_ada[None], 0, kv_b_ada, 2)

    h = _modulate(xf, sh_m0, sc_m0, seq)
    w_in_t = jnp.swapaxes(gdn_w_in, 1, 2)
    proj = _ws_mm(h, w_in_t, 0, main_cols, F32, transposed=True)
    gate_logits = _narrow_mm(h, w_in_t, 0, main_cols, 2 * n_vh)
    prm = jnp.zeros((8, LANES), F32)
    prm = prm.at[0, n_vh:2 * n_vh].set(gdn_dt_bias[0].astype(F32))
    prm = prm.at[1, n_vh:2 * n_vh].set(-jnp.exp(gdn_a_log[0].astype(F32)))
    conv_wt = jnp.transpose(gdn_conv_w[0])
    lg3 = gate_logits.reshape(bsz, seq, LANES)
    lgt = jnp.transpose(lg3[:, :, n_vh:2 * n_vh], (0, 2, 1)).reshape(bsz, n_vh, seq // GDN_CHUNK, GDN_CHUNK)
    prow = jnp.broadcast_to(prm[0:2, n_vh:2 * n_vh].T[:, :, None], (n_vh, 2, GDN_CHUNK))
    o = _gdn(proj.reshape(bsz, seq, main_cols), lg3, lgt, prm, prow, conv_wt,
             gdn_norm_w[0].reshape(1, HEAD_DIM), n_qk)
    x1, h1 = _proj_ln(o.reshape(n, value_dim), gdn_w_out[0].astype(BF16), xf, gt_m0, ln_g[0, 0], ln_b[0, 0],
                      [(sh_f0, sc_f0)], seq)

    hm = _ws_swiglu(h1, ffn_w_gu, 0, tm=seq)
    x2, h2, h_kv = _proj_ln(hm, ffn_w_down[0].astype(BF16), x1, gt_f0, ln_g[0, 1], ln_b[0, 1],
                            [(sh_m1, sc_m1), (sh_kv, sc_kv)], seq)

    kv = _ws_mm(h_kv, sb_w_kv[None], 0, 2 * n_sb_heads * HEAD_DIM, BF16, tm=seq)
    q = _ws_mm(h2, sb_w_q, 0, n_sb_heads * HEAD_DIM, BF16, tm=seq)
    oa = _sb_attention(q.reshape(bsz, seq, -1), kv.reshape(bsz, seq, -1), n_sb_heads)
    w_router = jnp.zeros((d, LANES), F32).at[:, :N_EXPERTS].set(moe_w_router[0])
    x3, h3, logits = _proj_ln(oa.reshape(n, -1), sb_w_out[0].astype(BF16), x2, gt_m1, ln_g[1, 0], ln_b[1, 0],
                              [(sh_f1, sc_f1)], seq, h_dtype=F32, w_router=w_router)

    b_router = jnp.zeros((1, LANES), F32).at[0, :N_EXPERTS].set(moe_b_router[0].astype(F32))
    idx, wts, cnt = _route(logits, b_router)
    tm_g = MOE_TILE
    n_tiles = 2 * n // tm_g + N_EXPERTS
    counts = cnt[0, :N_EXPERTS].astype(jnp.int32)
    tiles_per = (counts + tm_g - 1) // tm_g
    tile_end = jnp.cumsum(tiles_per)
    tile_first = tile_end - tiles_per
    row_off = tile_first * tm_g
    tile_ids = jnp.arange(n_tiles, dtype=jnp.int32)
    tile_expert = jnp.minimum(jnp.sum((tile_ids[:, None] >= tile_end[None, :]).astype(jnp.int32), axis=1),
                              N_EXPERTS - 1)
    rows_left = counts[tile_expert] - (tile_ids - tile_first[tile_expert]) * tm_g
    tile_rows = jnp.where(tile_ids < tile_end[-1], jnp.clip(rows_left, 0, tm_g), 0)
    tile_steps = ((tile_rows + MOE_ROW_STEP - 1) // MOE_ROW_STEP).astype(jnp.int32)
    pieces = tm_g // MOE_ROW_STEP
    piece_used = (jnp.arange(pieces, dtype=jnp.int32)[None, :] < tile_steps[:, None]).astype(jnp.int32).reshape(-1)
    dest1 = row_off[idx[:, 0]] + idx[:, 2]
    dest2 = row_off[idx[:, 1]] + idx[:, 3]
    tok = jnp.arange(n, dtype=jnp.int32)
    src = jnp.zeros((n_tiles * tm_g,), jnp.int32).at[jnp.concatenate([dest1, dest2])].set(
        jnp.concatenate([tok, tok]), unique_indices=True)

    xs = _gather_rows(h3, src, piece_used)
    hm2 = _group_swiglu(xs, moe_w_gu, 0, tile_expert, tile_steps, tm_g)
    o_sorted = _group_down(hm2, moe_w_down[0], tile_expert, tile_steps, tm_g)
    out = _combine_ln(o_sorted, dest1, dest2, wts, x3, gt_f1, ln_g[1, 1], ln_b[1, 1], seq)
    return out.reshape(bsz, seq, d)
```

```python
import functools

import jax
import jax.numpy as jnp
from jax import lax
from jax.experimental import pallas as pl
from jax.experimental.pallas import tpu as pltpu

F32 = jnp.float32
BF16 = jnp.bfloat16

LANES = 128
VMEM_LIMIT_BYTES = 56 * 1024 * 1024

HEAD_DIM = 128
GDN_CHUNK = 64
CONV_WIDTH = 4
N_EXPERTS = 8
MOE_TILE = 1024
MOE_ROW_STEP = 256
DMA_BURST = 8
LN_EPS = 1e-5
RMS_EPS = 1e-6
L2_EPS = 1e-6
DEPTH = 2
DN_ALPHA = (2.0 * DEPTH) ** 0.25


def _cparams(*sem):
    return pltpu.CompilerParams(dimension_semantics=sem, vmem_limit_bytes=VMEM_LIMIT_BYTES)


def _dot(a, b):
    return jnp.dot(a, b, preferred_element_type=F32)


def _dot_nt(a, b):
    return lax.dot_general(a, b, (((1,), (1,)), ((), ())), preferred_element_type=F32)


def _dot_tn(a, b):
    return lax.dot_general(a, b, (((0,), (0,)), ((), ())), preferred_element_type=F32)


def _split(a):
    hi = a.astype(BF16)
    lo = (a - hi.astype(F32)).astype(BF16)
    return hi, lo


def _dot3(a, b):
    ah, al = _split(a)
    bh, bl = _split(b)
    return _dot(ah, bh) + (_dot(ah, bl) + _dot(al, bh))


def _softplus(x):
    return jnp.maximum(x, 0.0) + jnp.log(1.0 + jnp.exp(-jnp.abs(x)))


def _silu(x):
    return x * jax.nn.sigmoid(x)


def _ada_kernel(c_ref, w_ref, b_ref, o_ref):
    s = _silu(c_ref[...]).astype(BF16)
    o_ref[...] = _dot(s, w_ref[...].astype(BF16)) + b_ref[...]


def _ada(c_pad, w, layer, b, tn=1024):
    _, d, n_out = w.shape
    return pl.pallas_call(
        _ada_kernel,
        grid=(n_out // tn,),
        in_specs=[pl.BlockSpec((8, d), lambda n: (0, 0)),
                  pl.BlockSpec((None, d, tn), lambda n: (layer, 0, n)),
                  pl.BlockSpec((1, tn), lambda n: (0, n))],
        out_specs=pl.BlockSpec((8, tn), lambda n: (0, n)),
        out_shape=jax.ShapeDtypeStruct((8, n_out), F32),
        compiler_params=_cparams("arbitrary"),
    )(c_pad, w, b.reshape(1, n_out))


def _modulate_kernel(x_ref, sh_ref, sc_ref, o_ref):
    o_ref[...] = (x_ref[...] * (1.0 + sc_ref[0]) + sh_ref[0]).astype(o_ref.dtype)


def _modulate(xf, shift, scale, seq, tm=512):
    n, d = xf.shape
    bmap = lambda i: (i * tm // seq, 0, 0)
    return pl.pallas_call(
        _modulate_kernel,
        grid=(n // tm,),
        in_specs=[pl.BlockSpec((tm, d), lambda i: (i, 0)),
                  pl.BlockSpec((1, 1, d), bmap),
                  pl.BlockSpec((1, 1, d), bmap)],
        out_specs=pl.BlockSpec((tm, d), lambda i: (i, 0)),
        out_shape=jax.ShapeDtypeStruct((n, d), BF16),
        compiler_params=_cparams("arbitrary"),
    )(xf, shift, scale)


def _ws_mm_kernel(a_ref, w_ref, o_ref, wb_ref, *, transposed):
    @pl.when(pl.program_id(1) == 0)
    def _():
        w = w_ref[...]
        wb_ref[...] = (w.T if transposed else w).astype(BF16)

    o_ref[...] = _dot(a_ref[...], wb_ref[...]).astype(o_ref.dtype)


def _ws_mm(a, w, layer, n_out, out_dtype, transposed=False, tm=1024, tn=1024):
    m, k = a.shape
    tn = min(tn, n_out)
    if transposed:
        w_spec = pl.BlockSpec((None, tn, k), lambda n, i: (layer, n, 0))
    else:
        w_spec = pl.BlockSpec((None, k, tn), lambda n, i: (layer, 0, n))
    return pl.pallas_call(
        functools.partial(_ws_mm_kernel, transposed=transposed),
        grid=(n_out // tn, m // tm),
        in_specs=[pl.BlockSpec((tm, k), lambda n, i: (i, 0)), w_spec],
        out_specs=pl.BlockSpec((tm, tn), lambda n, i: (i, n)),
        out_shape=jax.ShapeDtypeStruct((m, n_out), out_dtype),
        scratch_shapes=[pltpu.VMEM((k, tn), BF16)],
        compiler_params=_cparams("arbitrary", "arbitrary"),
    )(a, w)


def _narrow_mm_kernel(a_ref, w_hbm, o_ref, wf_ref, wb_ref, sem, *, layer, row0, n_rows):
    @pl.when(pl.program_id(0) == 0)
    def _():
        copy = pltpu.make_async_copy(w_hbm.at[layer, pl.ds(row0, n_rows), :], wf_ref, sem)
        copy.start()
        copy.wait()
        wb_ref[...] = jnp.zeros_like(wb_ref)
        wb_ref[:n_rows, :] = wf_ref[...].astype(BF16)

    o_ref[...] = _dot_nt(a_ref[...], wb_ref[...])


def _narrow_mm(a, w_t, layer, row0, n_rows, tm=1024):
    m, k = a.shape
    return pl.pallas_call(
        functools.partial(_narrow_mm_kernel, layer=layer, row0=row0, n_rows=n_rows),
        grid=(m // tm,),
        in_specs=[pl.BlockSpec((tm, k), lambda i: (i, 0)),
                  pl.BlockSpec(memory_space=pl.ANY)],
        out_specs=pl.BlockSpec((tm, LANES), lambda i: (i, 0)),
        out_shape=jax.ShapeDtypeStruct((m, LANES), F32),
        scratch_shapes=[pltpu.VMEM((n_rows, k), F32), pltpu.VMEM((LANES, k), BF16), pltpu.SemaphoreType.DMA(())],
        compiler_params=_cparams("arbitrary"),
    )(a, w_t)


def _ws_swiglu_kernel(a_ref, wg_ref, wu_ref, o_ref, wgb_ref, wub_ref):
    @pl.when(pl.program_id(1) == 0)
    def _():
        wgb_ref[...] = wg_ref[...].astype(BF16)
        wub_ref[...] = wu_ref[...].astype(BF16)

    a = a_ref[...]
    g = _dot(a, wgb_ref[...])
    u = _dot(a, wub_ref[...])
    o_ref[...] = (_silu(g) * u).astype(o_ref.dtype)


def _ws_swiglu(a, w_gu, layer, tm=1024, tn=512):
    m, k = a.shape
    f = w_gu.shape[-1] // 2
    nt = f // tn
    return pl.pallas_call(
        _ws_swiglu_kernel,
        grid=(nt, m // tm),
        in_specs=[pl.BlockSpec((tm, k), lambda n, i: (i, 0)),
                  pl.BlockSpec((None, k, tn), lambda n, i: (layer, 0, n)),
                  pl.BlockSpec((None, k, tn), lambda n, i: (layer, 0, nt + n))],
        out_specs=pl.BlockSpec((tm, tn), lambda n, i: (i, n)),
        out_shape=jax.ShapeDtypeStruct((m, f), BF16),
        scratch_shapes=[pltpu.VMEM((k, tn), BF16), pltpu.VMEM((k, tn), BF16)],
        compiler_params=_cparams("arbitrary", "arbitrary"),
    )(a, w_gu, w_gu)


def _layer_norm(r, g, b):
    mu = jnp.mean(r, axis=-1, keepdims=True)
    rc = r - mu
    var = jnp.mean(rc * rc, axis=-1, keepdims=True)
    return rc * lax.rsqrt(var + LN_EPS) * g + b


def _proj_ln_kernel(*refs, n_mod, nk, with_router):
    a_ref, w_ref, x_ref, gate_ref, g_ref, b_ref = refs[:6]
    mod_refs = refs[6:6 + 2 * n_mod]
    pos = 6 + 2 * n_mod
    wr_ref = refs[pos] if with_router else None
    pos += int(with_router)
    xo_ref = refs[pos]
    h_refs = refs[pos + 1:pos + 1 + n_mod]
    pos += 1 + n_mod
    lg_ref = refs[pos] if with_router else None
    acc_ref = refs[-1]
    kk = pl.program_id(1)

    @pl.when(kk == 0)
    def _():
        acc_ref[...] = jnp.zeros_like(acc_ref)

    acc_ref[...] += _dot(a_ref[...], w_ref[...])

    @pl.when(kk == nk - 1)
    def _():
        r = DN_ALPHA * x_ref[...] + gate_ref[0] * acc_ref[...]
        xn = _layer_norm(r, g_ref[...], b_ref[...])
        xo_ref[...] = xn
        for t in range(n_mod):
            h = xn * (1.0 + mod_refs[2 * t + 1][0]) + mod_refs[2 * t][0]
            h_refs[t][...] = h.astype(h_refs[t].dtype)
            if with_router and t == 0:
                lg_ref[...] = _dot3(h, wr_ref[...])


def _k_tile(k, limit=2048, unit=256):
    nk = 1
    while k % nk or k // nk > limit or (k // nk) % unit:
        nk += 1
    return k // nk


def _proj_ln(a, w, xf, gate, ln_g, ln_b, mods, seq, h_dtype=BF16, w_router=None, tm=512):
    m, k = a.shape
    d = w.shape[-1]
    tk = _k_tile(k)
    nk = k // tk
    n_mod = len(mods)
    with_router = w_router is not None
    bmap = lambda i, kk: (i * tm // seq, 0, 0)
    row = lambda i, kk: (i, 0)
    const = lambda i, kk: (0, 0)
    in_specs = [pl.BlockSpec((tm, tk), lambda i, kk: (i, kk)),
                pl.BlockSpec((tk, d), lambda i, kk: (kk, 0)),
                pl.BlockSpec((tm, d), row),
                pl.BlockSpec((1, 1, d), bmap),
                pl.BlockSpec((1, d), const),
                pl.BlockSpec((1, d), const)]
    args = [a, w, xf, gate, ln_g.reshape(1, d), ln_b.reshape(1, d)]
    for sh, sc in mods:
        in_specs += [pl.BlockSpec((1, 1, d), bmap), pl.BlockSpec((1, 1, d), bmap)]
        args += [sh, sc]
    out_specs = [pl.BlockSpec((tm, d), row)] + [pl.BlockSpec((tm, d), row)] * n_mod
    out_shape = [jax.ShapeDtypeStruct((m, d), F32)] + [jax.ShapeDtypeStruct((m, d), h_dtype)] * n_mod
    if with_router:
        in_specs.append(pl.BlockSpec((d, LANES), const))
        args.append(w_router)
        out_specs.append(pl.BlockSpec((tm, LANES), row))
        out_shape.append(jax.ShapeDtypeStruct((m, LANES), F32))
    return pl.pallas_call(
        functools.partial(_proj_ln_kernel, n_mod=n_mod, nk=nk, with_router=with_router),
        grid=(m // tm, nk),
        in_specs=in_specs,
        out_specs=out_specs,
        out_shape=out_shape,
        scratch_shapes=[pltpu.VMEM((tm, d), F32)],
        compiler_params=_cparams("arbitrary", "arbitrary"),
    )(*args)


GDN_GROUP = 8


def _gdn_kernel(q_ref, k_ref, v_ref, z_ref, cwq_ref, cwk_ref, cwv_ref, lg_ref, lgt_ref, prm_ref, prow_ref, nw_ref,
                o_ref, gcr_s, kw_r, nn_r, qp_r, op_r, dg_r):
    j = pl.program_id(1)
    seq = q_ref.shape[1]
    c = GDN_CHUNK
    hd = HEAD_DIM
    n_groups = seq // (c * GDN_GROUP)
    n_vh = 2 * pl.num_programs(1)
    halo = 8

    def window(ref, ch):
        if isinstance(ch, int) and ch == 0:
            body = ref[0, 0:c, :]
            return jnp.concatenate([jnp.zeros((halo, body.shape[1]), F32), body], axis=0)
        if isinstance(ch, int):
            return ref[0, ch * c - halo:(ch + 1) * c, :]
        return ref[0, pl.ds(pl.multiple_of(ch * c - halo, halo), c + halo), :]

    def conv_silu(xw, cw):
        acc = xw[halo:, :] * cw[CONV_WIDTH - 1:CONV_WIDTH, :]
        for s in range(1, CONV_WIDTH):
            acc = acc + xw[halo - s:halo - s + c, :] * cw[CONV_WIDTH - 1 - s:CONV_WIDTH - s, :]
        return _silu(acc)

    def l2norm(x):
        return x * lax.rsqrt(jnp.sum(x * x, axis=-1, keepdims=True) + L2_EPS)

    ri = lax.broadcasted_iota(jnp.int32, (c, c), 0)
    ci = lax.broadcasted_iota(jnp.int32, (c, c), 1)
    upper = (ri <= ci).astype(BF16)
    for hh in range(2):
        g_row = prow_ref[hh, 1:2, :] * _softplus(lgt_ref[0, hh] + prow_ref[hh, 0:1, :])
        g1, g2 = _split(g_row)
        g3 = (g_row - g1.astype(F32) - g2.astype(F32)).astype(BF16)
        gcr_s[hh] = _dot(g1, upper) + (_dot(g2, upper) + _dot(g3, upper))

    incl = ri >= ci
    strict = ri > ci
    lane = lax.broadcasted_iota(jnp.int32, (c, LANES), 1)
    pos = lax.broadcasted_iota(jnp.int32, (c, 1), 0)

    def select_lane(x, idx):
        return jnp.sum(jnp.where(lane == idx, x, 0.0), axis=1, keepdims=True)

    def prepare_group(first_chunk, slot):
        chunks = []
        for k in range(GDN_GROUP):
            ch = first_chunk + k
            rows = pl.ds(ch * c, c) if isinstance(ch, int) else pl.ds(pl.multiple_of(ch * c, c), c)
            qc = l2norm(conv_silu(window(q_ref, ch), cwq_ref[...])) * (hd ** -0.5)
            kc = l2norm(conv_silu(window(k_ref, ch), cwk_ref[...]))
            vc = conv_silu(window(v_ref, ch), cwv_ref[...])
            chunks.append((ch, rows, kc, qc, vc, kc.astype(BF16), qc.astype(BF16)))
        kks = [_dot_nt(kcb, kcb) for (_, _, _, _, _, kcb, _) in chunks]
        qks = [_dot_nt(qcb, kcb) for (_, _, _, _, _, kcb, qcb) in chunks]
        probs = []
        for (ch, rows, kc, qc, vc, _, _), kk, qk in zip(chunks, kks, qks):
            lgc = lg_ref[0, rows, :]
            gac = prm_ref[1:2, :] * _softplus(lgc + prm_ref[0:1, :])
            for hh in range(2):
                head = 2 * j + hh
                beta = jax.nn.sigmoid(select_lane(lgc, head))
                gc = jnp.broadcast_to(select_lane(gac, n_vh + head), (c, hd))
                s = 1
                while s < c:
                    gc = gc + jnp.where(pos >= s, pltpu.roll(gc, s, axis=0), 0.0)
                    s *= 2
                gl = jnp.broadcast_to(gc[c - 1:c, :], (c, hd))
                eg = jnp.exp(gc)
                row_b = jnp.broadcast_to(gcr_s[hh, pl.ds(ch, 1), :], (c, c))
                dec = jnp.exp(jnp.where(incl, gc[:, :c] - row_b, -jnp.inf))
                low = jnp.where(strict, kk * beta * dec, 0.0)
                rhs = jnp.concatenate([vc[:, hh * hd:(hh + 1) * hd] * beta, kc * beta * eg], axis=1)
                probs.append(dict(low=low, rhs=rhs, at=(qk * dec).astype(BF16), qd=qc * eg,
                                  kd=(kc * jnp.exp(gl - gc)).astype(BF16), dg=jnp.exp(gl[:8, :])))
        yield
        eye = (ri == ci).astype(F32)
        ps = [p["low"] for p in probs]
        xs = [eye - p["low"] for p in probs]
        n_terms = 2
        while n_terms < c:
            pbs = [p.astype(BF16) for p in ps]
            ps = [_dot(pb, pb) for pb in pbs]
            xs = [x + _dot(x.astype(BF16), p.astype(BF16)) for x, p in zip(xs, ps)]
            n_terms *= 2
            if n_terms in (4, 16, 64):
                yield
        sols = [_dot(x.astype(BF16), p["rhs"].astype(BF16)) for x, p in zip(xs, probs)]
        ubs = [sol[:, :hd].astype(BF16) for sol in sols]
        wbs = [sol[:, hd:].astype(BF16) for sol in sols]
        kws = [_dot_tn(p["kd"], wb) for p, wb in zip(probs, wbs)]
        nns = [_dot_tn(p["kd"], ub) for p, ub in zip(probs, ubs)]
        aws = [_dot(p["at"], wb) for p, wb in zip(probs, wbs)]
        ops = [_dot(p["at"], ub) for p, ub in zip(probs, ubs)]
        for n, p in enumerate(probs):
            k, hh = divmod(n, 2)
            kw_r[slot, k, hh] = kws[n].astype(BF16)
            nn_r[slot, k, hh] = nns[n]
            qp_r[slot, k, hh] = (p["qd"] - aws[n]).astype(BF16)
            op_r[slot, k, hh] = ops[n]
            dg_r[slot, k, hh] = p["dg"]

    def scan(ch, slot, k, states):
        rows = pl.ds(pl.multiple_of(ch * c, c), c)
        new_states = []
        for hh in range(2):
            st = states[hh]
            sb = st.astype(BF16)
            o = _dot(qp_r[slot, k, hh], sb) + op_r[slot, k, hh]
            decay = jnp.broadcast_to(dg_r[slot, k, hh][0:1, :], (hd, hd))
            new_states.append(st * decay - _dot(kw_r[slot, k, hh], sb) + nn_r[slot, k, hh])
            o = o * lax.rsqrt(jnp.mean(o * o, axis=-1, keepdims=True) + RMS_EPS) * nw_ref[...]
            cols = slice(hh * hd, (hh + 1) * hd)
            o_ref[0, rows, cols] = (o * _silu(z_ref[0, rows, cols])).astype(o_ref.dtype)
        return tuple(new_states)

    for _ in prepare_group(0, 0):
        pass

    def body(g, states):
        slot = g % 2
        stages = prepare_group((g + 1) * GDN_GROUP, 1 - slot)
        for k in range(GDN_GROUP):
            if k % (GDN_GROUP // 4) == 0:
                next(stages)
            states = scan(g * GDN_GROUP + k, slot, k, states)
        for _ in stages:
            pass
        return states

    zero = jnp.zeros((hd, hd), F32)
    states = lax.fori_loop(0, n_groups - 1, body, (zero, zero))
    last = n_groups - 1
    for k in range(GDN_GROUP):
        states = scan(last * GDN_GROUP + k, last % 2, k, states)


def _gdn(proj, lg, lgt, prm, prow, conv_wt, norm_w, n_qk_heads):
    bsz, seq, _ = proj.shape
    hq = n_qk_heads
    hd = HEAD_DIM
    n_chunks = seq // GDN_CHUNK
    ring = (2, GDN_GROUP, 2)
    return pl.pallas_call(
        _gdn_kernel,
        grid=(bsz, hq),
        in_specs=[pl.BlockSpec((1, seq, hd), lambda b, j: (b, 0, j)),
                  pl.BlockSpec((1, seq, hd), lambda b, j: (b, 0, hq + j)),
                  pl.BlockSpec((1, seq, 2 * hd), lambda b, j: (b, 0, hq + j)),
                  pl.BlockSpec((1, seq, 2 * hd), lambda b, j: (b, 0, 2 * hq + j)),
                  pl.BlockSpec((CONV_WIDTH, hd), lambda b, j: (0, j)),
                  pl.BlockSpec((CONV_WIDTH, hd), lambda b, j: (0, hq + j)),
                  pl.BlockSpec((CONV_WIDTH, 2 * hd), lambda b, j: (0, hq + j)),
                  pl.BlockSpec((1, seq, LANES), lambda b, j: (b, 0, 0)),
                  pl.BlockSpec((1, 2, n_chunks, GDN_CHUNK), lambda b, j: (b, j, 0, 0)),
                  pl.BlockSpec((8, LANES), lambda b, j: (0, 0)),
                  pl.BlockSpec((2, 2, GDN_CHUNK), lambda b, j: (j, 0, 0)),
                  pl.BlockSpec((1, hd), lambda b, j: (0, 0))],
        out_specs=pl.BlockSpec((1, seq, 2 * hd), lambda b, j: (b, 0, j)),
        out_shape=jax.ShapeDtypeStruct((bsz, seq, 2 * hq * hd), BF16),
        scratch_shapes=[pltpu.VMEM((2, n_chunks, GDN_CHUNK), F32),
                        pltpu.VMEM(ring + (hd, hd), BF16),
                        pltpu.VMEM(ring + (hd, hd), F32),
                        pltpu.VMEM(ring + (GDN_CHUNK, hd), BF16),
                        pltpu.VMEM(ring + (GDN_CHUNK, hd), F32),
                        pltpu.VMEM(ring + (8, hd), F32)],
        compiler_params=_cparams("arbitrary", "arbitrary"),
    )(proj, proj, proj, proj, conv_wt, conv_wt, conv_wt, lg, lgt, prm, prow, norm_w)


SB_HEADS_PER_STEP = 4
SB_ZERO_EXPONENT = -110.0


def _sb_kernel(q_ref, k_ref, v_ref, o_ref, kmax_s, *, blk):
    i = pl.program_id(2)
    hd = HEAD_DIM
    scale = hd ** -0.5
    heads = range(SB_HEADS_PER_STEP)

    @pl.when(i == 0)
    def _():
        for hh in heads:
            kf = k_ref[0, :, hh * hd:(hh + 1) * hd].astype(F32)
            n2 = jnp.max(jnp.sum(kf * kf, axis=1, keepdims=True), axis=0, keepdims=True)
            kmax_s[hh] = jnp.broadcast_to(jnp.sqrt(n2) * scale, kmax_s.shape[1:])

    ri = lax.broadcasted_iota(jnp.int32, (blk, blk), 0)
    ci = lax.broadcasted_iota(jnp.int32, (blk, blk), 1)
    tri = (ri >= ci).astype(BF16)
    causal = ci < ri
    qs = [q_ref[0, :, hh * hd:(hh + 1) * hd] for hh in heads]
    z_bound = []
    for hh in heads:
        qf = qs[hh].astype(F32)
        z_bound.append(jnp.sqrt(jnp.sum(qf * qf, axis=1, keepdims=True)) * kmax_s[hh][0:1, 0:1])

    def block(jb, accs, runs, diagonal):
        rows = pl.ds(pl.multiple_of(jb * blk, blk), blk)
        cols = [slice(hh * hd, (hh + 1) * hd) for hh in heads]
        zs = [_dot_nt(qs[hh], k_ref[0, rows, cols[hh]]) * scale for hh in heads]
        lss = [-_softplus(z) for z in zs]
        if diagonal:
            lss = [jnp.where(causal, ls, 0.0) for ls in lss]
        parts = [_split(ls) for ls in lss]
        rcs = [_dot(hi, tri) + _dot(lo, tri) for hi, lo in parts]
        ws = [jnp.exp(zs[hh] + rcs[hh] + runs[hh]) for hh in heads]
        if diagonal:
            ws = [jnp.where(causal, w, 0.0) for w in ws]
        new_accs = [accs[hh] + _dot(ws[hh].astype(BF16), v_ref[0, rows, cols[hh]]) for hh in heads]
        new_runs = [runs[hh] + rcs[hh][:, :1] for hh in heads]
        return tuple(new_accs), tuple(new_runs)

    def any_weight_left(runs):
        m = jnp.max(runs[0] + z_bound[0])
        for hh in heads[1:]:
            m = jnp.maximum(m, jnp.max(runs[hh] + z_bound[hh]))
        return (m > SB_ZERO_EXPONENT).astype(jnp.int32)

    zeros = tuple(jnp.zeros((blk, hd), F32) for _ in heads)
    zrun = tuple(jnp.zeros((blk, 1), F32) for _ in heads)
    accs, runs = block(i, zeros, zrun, True)

    def cond(carry):
        return jnp.logical_and(carry[0] >= 0, carry[1] > 0)

    def body(carry):
        jb, _, accs, runs = carry
        accs, runs = block(jb, accs, runs, False)
        return jb - 1, any_weight_left(runs), accs, runs

    _, _, accs, _ = lax.while_loop(cond, body, (i - 1, any_weight_left(runs), accs, runs))
    for hh in heads:
        o_ref[0, :, hh * hd:(hh + 1) * hd] = accs[hh].astype(o_ref.dtype)


def _sb_attention(q, kv, n_heads, blk=256):
    bsz, seq, _ = q.shape
    w = SB_HEADS_PER_STEP * HEAD_DIM
    n_hp = n_heads // SB_HEADS_PER_STEP
    return pl.pallas_call(
        functools.partial(_sb_kernel, blk=blk),
        grid=(bsz, n_hp, seq // blk),
        in_specs=[pl.BlockSpec((1, blk, w), lambda b, h, i: (b, i, h)),
                  pl.BlockSpec((1, seq, w), lambda b, h, i: (b, 0, h)),
                  pl.BlockSpec((1, seq, w), lambda b, h, i: (b, 0, n_hp + h))],
        out_specs=pl.BlockSpec((1, blk, w), lambda b, h, i: (b, i, h)),
        out_shape=jax.ShapeDtypeStruct((bsz, seq, n_heads * HEAD_DIM), BF16),
        scratch_shapes=[pltpu.VMEM((SB_HEADS_PER_STEP, 8, LANES), F32)],
        compiler_params=_cparams("arbitrary", "arbitrary", "arbitrary"),
    )(q, kv, kv)


def _route_kernel(lg_ref, br_ref, idx_ref, wt_ref, cnt_ref, carry_ref, *, tm):
    i = pl.program_id(0)

    @pl.when(i == 0)
    def _():
        carry_ref[...] = jnp.zeros_like(carry_ref)

    lane = lax.broadcasted_iota(jnp.int32, (tm, LANES), 1)
    lane_f = lane.astype(F32)
    logits = jnp.where(lane < N_EXPERTS, lg_ref[...] + br_ref[...], -jnp.inf)
    m1 = jnp.max(logits, axis=1, keepdims=True)
    i1 = jnp.min(jnp.where(logits == m1, lane_f, float(LANES)), axis=1, keepdims=True)
    oh1 = lane_f == i1
    rest = jnp.where(oh1, -jnp.inf, logits)
    m2 = jnp.max(rest, axis=1, keepdims=True)
    i2 = jnp.min(jnp.where(rest == m2, lane_f, float(LANES)), axis=1, keepdims=True)
    oh2 = lane_f == i2
    e21 = jnp.exp(m2 - m1)
    w1 = 1.0 / (1.0 + e21)
    w2 = e21 / (1.0 + e21)

    cnt = jnp.where(oh1, 1.0, jnp.where(oh2, 1.0, 0.0))
    before = (lax.broadcasted_iota(jnp.int32, (tm, tm), 1) < lax.broadcasted_iota(jnp.int32, (tm, tm), 0)).astype(BF16)
    rank = _dot(before, cnt.astype(BF16)) + carry_ref[0:1, :]
    r1 = jnp.sum(jnp.where(oh1, rank, 0.0), axis=1, keepdims=True)
    r2 = jnp.sum(jnp.where(oh2, rank, 0.0), axis=1, keepdims=True)
    total = carry_ref[0:1, :] + jnp.sum(cnt, axis=0, keepdims=True)
    carry_ref[...] = jnp.broadcast_to(total, carry_ref.shape)
    cnt_ref[...] = jnp.broadcast_to(total, cnt_ref.shape)

    packed = jnp.where(lane == 0, i1, jnp.where(lane == 1, i2, jnp.where(lane == 2, r1, jnp.where(lane == 3, r2, 0.0))))
    idx_ref[...] = packed.astype(jnp.int32)
    wt_ref[...] = jnp.where(lane == 0, w1, jnp.where(lane == 1, w2, 0.0))


def _route(logits, b_router_pad, tm=512):
    n = logits.shape[0]
    row = lambda i: (i, 0)
    return pl.pallas_call(
        functools.partial(_route_kernel, tm=tm),
        grid=(n // tm,),
        in_specs=[pl.BlockSpec((tm, LANES), row), pl.BlockSpec((1, LANES), lambda i: (0, 0))],
        out_specs=[pl.BlockSpec((tm, LANES), row), pl.BlockSpec((tm, LANES), row),
                   pl.BlockSpec((8, LANES), lambda i: (0, 0))],
        out_shape=[jax.ShapeDtypeStruct((n, LANES), jnp.int32), jax.ShapeDtypeStruct((n, LANES), F32),
                   jax.ShapeDtypeStruct((8, LANES), F32)],
        scratch_shapes=[pltpu.VMEM((8, LANES), F32)],
        compiler_params=_cparams("arbitrary"),
    )(logits, b_router_pad)


def _row_copy(src_hbm, src_row, dst, dst_row, sem):
    return pltpu.make_async_copy(src_hbm.at[pl.ds(src_row, 1)], dst.at[pl.ds(dst_row, 1)], sem)


def _gather_kernel(src_ref, used_ref, h_hbm, o_ref, buf, sem, *, tg):
    i = pl.program_id(0)
    base = i * tg

    @pl.when(used_ref[i] > 0)
    def _():
        def start(g, carry):
            for k in range(DMA_BURST):
                r = g * DMA_BURST + k
                _row_copy(h_hbm, src_ref[base + r], buf, r, sem).start(priority=k % 2)
            return carry

        lax.fori_loop(0, tg // DMA_BURST, start, 0)
        pltpu.make_async_copy(h_hbm.at[pl.ds(0, tg)], buf, sem).wait()
        o_ref[...] = buf[...].astype(o_ref.dtype)

    @pl.when(used_ref[i] == 0)
    def _():
        o_ref[...] = jnp.zeros_like(o_ref)


def _gather_rows(h, src, piece_used):
    tg = MOE_ROW_STEP
    m_pad = src.shape[0]
    d = h.shape[1]
    return pl.pallas_call(
        functools.partial(_gather_kernel, tg=tg),
        grid_spec=pltpu.PrefetchScalarGridSpec(
            num_scalar_prefetch=2,
            grid=(m_pad // tg,),
            in_specs=[pl.BlockSpec(memory_space=pl.ANY)],
            out_specs=pl.BlockSpec((tg, d), lambda i, src, used: (i, 0)),
            scratch_shapes=[pltpu.VMEM((tg, d), h.dtype), pltpu.SemaphoreType.DMA(())]),
        out_shape=jax.ShapeDtypeStruct((m_pad, d), BF16),
        compiler_params=_cparams("arbitrary"),
    )(src, piece_used, h)


def _group_swiglu_kernel(te_ref, tv_ref, a_ref, wg_ref, wu_ref, o_ref, wgb_ref, wub_ref):
    i = pl.program_id(1)
    new_expert = jnp.logical_or(i == 0, te_ref[i] != te_ref[jnp.maximum(i - 1, 0)])

    @pl.when(new_expert)
    def _():
        wgb_ref[...] = wg_ref[...].astype(BF16)
        wub_ref[...] = wu_ref[...].astype(BF16)

    tm = a_ref.shape[0]
    n_steps = tv_ref[i]

    def compute(n_rows):
        a = a_ref[:n_rows, :]
        g = _dot(a, wgb_ref[...])
        u = _dot(a, wub_ref[...])
        o_ref[:n_rows, :] = (_silu(g) * u).astype(o_ref.dtype)
        if n_rows < tm:
            o_ref[n_rows:, :] = jnp.zeros((tm - n_rows, o_ref.shape[1]), o_ref.dtype)

    for s in range(1, tm // MOE_ROW_STEP + 1):
        pl.when(n_steps == s)(functools.partial(compute, s * MOE_ROW_STEP))

    @pl.when(n_steps == 0)
    def _():
        o_ref[...] = jnp.zeros_like(o_ref)


def _group_swiglu(xs, w_gu, layer, tile_expert, tile_valid, tm, tn=512):
    m_pad, k = xs.shape
    f = w_gu.shape[-1] // 2
    nt = f // tn
    return pl.pallas_call(
        _group_swiglu_kernel,
        grid_spec=pltpu.PrefetchScalarGridSpec(
            num_scalar_prefetch=2,
            grid=(nt, m_pad // tm),
            in_specs=[pl.BlockSpec((tm, k), lambda n, i, te, tv: (jnp.where(tv[i] > 0, i, 0), 0)),
                      pl.BlockSpec((None, None, k, tn), lambda n, i, te, tv: (layer, te[i], 0, n)),
                      pl.BlockSpec((None, None, k, tn), lambda n, i, te, tv: (layer, te[i], 0, nt + n))],
            out_specs=pl.BlockSpec((tm, tn), lambda n, i, te, tv: (i, n)),
            scratch_shapes=[pltpu.VMEM((k, tn), BF16), pltpu.VMEM((k, tn), BF16)]),
        out_shape=jax.ShapeDtypeStruct((m_pad, f), BF16),
        compiler_params=_cparams("arbitrary", "arbitrary"),
    )(tile_expert, tile_valid, xs, w_gu, w_gu)


def _group_down_kernel(te_ref, tv_ref, a_ref, w_ref, o_ref, *, nk):
    i = pl.program_id(0)
    kk = pl.program_id(1)

    @pl.when(kk == 0)
    def _():
        o_ref[...] = jnp.zeros_like(o_ref)

    n_steps = tv_ref[i]

    def accumulate(n_rows):
        o_ref[:n_rows, :] += _dot(a_ref[:n_rows, :], w_ref[...].astype(BF16))

    for s in range(1, a_ref.shape[0] // MOE_ROW_STEP + 1):
        pl.when(n_steps == s)(functools.partial(accumulate, s * MOE_ROW_STEP))


def _group_down(hm, w_down, tile_expert, tile_valid, tm, tk=1024):
    m_pad, f = hm.shape
    d = w_down.shape[-1]
    nk = f // tk
    return pl.pallas_call(
        functools.partial(_group_down_kernel, nk=nk),
        grid_spec=pltpu.PrefetchScalarGridSpec(
            num_scalar_prefetch=2,
            grid=(m_pad // tm, nk),
            in_specs=[pl.BlockSpec((tm, tk), lambda i, kk, te, tv: (i, jnp.where(tv[i] > 0, kk, 0))),
                      pl.BlockSpec((None, tk, d), lambda i, kk, te, tv: (te[i], jnp.where(tv[i] > 0, kk, 0), 0))],
            out_specs=pl.BlockSpec((tm, d), lambda i, kk, te, tv: (i, 0))),
        out_shape=jax.ShapeDtypeStruct((m_pad, d), F32),
        compiler_params=_cparams("arbitrary", "arbitrary"),
    )(tile_expert, tile_valid, hm, w_down)


def _combine_ln_kernel(d1_ref, d2_ref, o_hbm, wt_ref, x_ref, gate_ref, g_ref, b_ref, xo_ref, buf1, buf2, sem, *, tm):
    base = pl.program_id(0) * tm

    def start(g, carry):
        for k in range(DMA_BURST):
            r = g * DMA_BURST + k
            _row_copy(o_hbm, d1_ref[base + r], buf1, r, sem).start(priority=0)
            _row_copy(o_hbm, d2_ref[base + r], buf2, r, sem).start(priority=1)
        return carry

    lax.fori_loop(0, tm // DMA_BURST, start, 0)
    pltpu.make_async_copy(o_hbm.at[pl.ds(0, tm)], buf1, sem).wait()
    pltpu.make_async_copy(o_hbm.at[pl.ds(0, tm)], buf2, sem).wait()
    wt = wt_ref[...]
    y = wt[:, 0:1] * buf1[...] + wt[:, 1:2] * buf2[...]
    r = DN_ALPHA * x_ref[...] + gate_ref[0] * y
    xo_ref[...] = _layer_norm(r, g_ref[...], b_ref[...])


def _combine_ln(o_sorted, d1, d2, wts, xf, gate, ln_g, ln_b, seq, tm=256):
    n, d = xf.shape
    row = lambda i, d1, d2: (i, 0)
    const = lambda i, d1, d2: (0, 0)
    return pl.pallas_call(
        functools.partial(_combine_ln_kernel, tm=tm),
        grid_spec=pltpu.PrefetchScalarGridSpec(
            num_scalar_prefetch=2,
            grid=(n // tm,),
            in_specs=[pl.BlockSpec(memory_space=pl.ANY),
                      pl.BlockSpec((tm, LANES), row),
                      pl.BlockSpec((tm, d), row),
                      pl.BlockSpec((1, 1, d), lambda i, d1, d2: (i * tm // seq, 0, 0)),
                      pl.BlockSpec((1, d), const),
                      pl.BlockSpec((1, d), const)],
            out_specs=pl.BlockSpec((tm, d), row),
            scratch_shapes=[pltpu.VMEM((tm, d), F32), pltpu.VMEM((tm, d), F32), pltpu.SemaphoreType.DMA(())]),
        out_shape=jax.ShapeDtypeStruct((n, d), F32),
        compiler_params=_cparams("arbitrary"),
    )(d1, d2, o_sorted, wts, xf, gate, ln_g.reshape(1, d), ln_b.reshape(1, d))


def kernel(x, c, w_ada, b_ada, ln_g, ln_b, gdn_w_in, gdn_conv_w, gdn_a_log, gdn_dt_bias, gdn_norm_w, gdn_w_out,
           kv_w_ada, kv_b_ada, sb_w_kv, sb_w_q, sb_w_out, ffn_w_gu, ffn_w_down, moe_w_router, moe_b_router,
           moe_w_gu, moe_w_down):
    bsz, seq, d = x.shape
    n = bsz * seq
    assert w_ada.shape[0] == DEPTH and gdn_w_in.shape[0] == 1 and sb_w_q.shape[0] == 1
    n_vh = gdn_a_log.shape[1]
    n_qk = n_vh // 2
    key_dim = n_qk * HEAD_DIM
    value_dim = n_vh * HEAD_DIM
    main_cols = 2 * key_dim + 2 * value_dim
    n_sb_heads = sb_w_q.shape[-1] // HEAD_DIM
    xf = x.reshape(n, d)

    c_pad = jnp.zeros((8, d), F32).at[:bsz].set(c)
    def mod_vectors(w, layer, b, parts):
        m = _ada(c_pad, w, layer, b)[:bsz]
        return [v.reshape(bsz, 1, d) for v in jnp.split(m, parts, axis=-1)]
    sh_m0, sc_m0, gt_m0, sh_f0, sc_f0, gt_f0 = mod_vectors(w_ada, 0, b_ada[0], 6)
    sh_m1, sc_m1, gt_m1, sh_f1, sc_f1, gt_f1 = mod_vectors(w_ada, 1, b_ada[1], 6)
    sh_kv, sc_kv = mod_vectors(kv_w_ada[None], 0, kv_b_ada, 2)

    h = _modulate(xf, sh_m0, sc_m0, seq)
    w_in_t = jnp.swapaxes(gdn_w_in, 1, 2)
    proj = _ws_mm(h, w_in_t, 0, main_cols, F32, transposed=True)
    gate_logits = _narrow_mm(h, w_in_t, 0, main_cols, 2 * n_vh)
    prm = jnp.zeros((8, LANES), F32)
    prm = prm.at[0, n_vh:2 * n_vh].set(gdn_dt_bias[0].astype(F32))
    prm = prm.at[1, n_vh:2 * n_vh].set(-jnp.exp(gdn_a_log[0].astype(F32)))
    conv_wt = jnp.transpose(gdn_conv_w[0])
    lg3 = gate_logits.reshape(bsz, seq, LANES)
    lgt = jnp.transpose(lg3[:, :, n_vh:2 * n_vh], (0, 2, 1)).reshape(bsz, n_vh, seq // GDN_CHUNK, GDN_CHUNK)
    prow = jnp.broadcast_to(prm[0:2, n_vh:2 * n_vh].T[:, :, None], (n_vh, 2, GDN_CHUNK))
    o = _gdn(proj.reshape(bsz, seq, main_cols), lg3, lgt, prm, prow, conv_wt,
             gdn_norm_w[0].reshape(1, HEAD_DIM), n_qk)
    x1, h1 = _proj_ln(o.reshape(n, value_dim), gdn_w_out[0].astype(BF16), xf, gt_m0, ln_g[0, 0], ln_b[0, 0],
                      [(sh_f0, sc_f0)], seq)

    hm = _ws_swiglu(h1, ffn_w_gu, 0, tm=seq)
    x2, h2, h_kv = _proj_ln(hm, ffn_w_down[0].astype(BF16), x1, gt_f0, ln_g[0, 1], ln_b[0, 1],
                            [(sh_m1, sc_m1), (sh_kv, sc_kv)], seq)

    kv = _ws_mm(h_kv, sb_w_kv[None], 0, 2 * n_sb_heads * HEAD_DIM, BF16, tm=seq)
    q = _ws_mm(h2, sb_w_q, 0, n_sb_heads * HEAD_DIM, BF16, tm=seq)
    oa = _sb_attention(q.reshape(bsz, seq, -1), kv.reshape(bsz, seq, -1), n_sb_heads)
    w_router = jnp.zeros((d, LANES), F32).at[:, :N_EXPERTS].set(moe_w_router[0])
    x3, h3, logits = _proj_ln(oa.reshape(n, -1), sb_w_out[0].astype(BF16), x2, gt_m1, ln_g[1, 0], ln_b[1, 0],
                              [(sh_f1, sc_f1)], seq, h_dtype=F32, w_router=w_router)

    b_router = jnp.zeros((1, LANES), F32).at[0, :N_EXPERTS].set(moe_b_router[0].astype(F32))
    idx, wts, cnt = _route(logits, b_router)
    tm_g = MOE_TILE
    n_tiles = 2 * n // tm_g + N_EXPERTS
    counts = cnt[0, :N_EXPERTS].astype(jnp.int32)
    tiles_per = (counts + tm_g - 1) // tm_g
    tile_end = jnp.cumsum(tiles_per)
    tile_first = tile_end - tiles_per
    row_off = tile_first * tm_g
    tile_ids = jnp.arange(n_tiles, dtype=jnp.int32)
    tile_expert = jnp.minimum(jnp.sum((tile_ids[:, None] >= tile_end[None, :]).astype(jnp.int32), axis=1),
                              N_EXPERTS - 1)
    rows_left = counts[tile_expert] - (tile_ids - tile_first[tile_expert]) * tm_g
    tile_rows = jnp.where(tile_ids < tile_end[-1], jnp.clip(rows_left, 0, tm_g), 0)
    tile_steps = ((tile_rows + MOE_ROW_STEP - 1) // MOE_ROW_STEP).astype(jnp.int32)
    pieces = tm_g // MOE_ROW_STEP
    piece_used = (jnp.arange(pieces, dtype=jnp.int32)[None, :] < tile_steps[:, None]).astype(jnp.int32).reshape(-1)
    dest1 = row_off[idx[:, 0]] + idx[:, 2]
    dest2 = row_off[idx[:, 1]] + idx[:, 3]
    tok = jnp.arange(n, dtype=jnp.int32)
    src = jnp.zeros((n_tiles * tm_g,), jnp.int32).at[jnp.concatenate([dest1, dest2])].set(
        jnp.concatenate([tok, tok]), unique_indices=True)

    xs = _gather_rows(h3, src, piece_used)
    hm2 = _group_swiglu(xs, moe_w_gu, 0, tile_expert, tile_steps, tm_g)
    o_sorted = _group_down(hm2, moe_w_down[0], tile_expert, tile_steps, tm_g)
    out = _combine_ln(o_sorted, dest1, dest2, wts, x3, gt_f1, ln_g[1, 1], ln_b[1, 1], seq)
    return out.reshape(bsz, seq, d)
```
